```python
import math
import jax, jax.numpy as jnp
from jax import lax
import numpy as np

D_MODEL = 1024
BATCH = 4
SEQ = 8192
DEPTH = 4

PLE_DIM = 256
N_A_LAYERS = DEPTH // 2
N_B_LAYERS = DEPTH - N_A_LAYERS
DIFF_HEAD_DIM = 64
DIFF_HEADS = D_MODEL // (2 * DIFF_HEAD_DIM)
FOX_HEAD_DIM = 64
FOX_HEADS = D_MODEL // FOX_HEAD_DIM
D_FF = 4 * D_MODEL
ROT_DIM = DIFF_HEAD_DIM // 4
ROPE_THETA = 500000.0
BLOCK_Q = 128
RMS_EPS = 1e-6

kernel_name = "yoco_diff_fox_hybrid"


def _rms_norm(x, gain):
    xf = x.astype(jnp.float32)
    y = xf * lax.rsqrt(jnp.mean(xf * xf, axis=-1, keepdims=True) + RMS_EPS)
    return (y * gain.astype(jnp.float32)).astype(x.dtype)


def _rope_tables(positions, dtype):
    inv_freq = 1.0 / (ROPE_THETA ** (jnp.arange(0, ROT_DIM, 2, dtype=jnp.float32) / ROT_DIM))
    ang = positions.astype(jnp.float32)[..., None] * inv_freq
    return jnp.cos(ang).astype(dtype), jnp.sin(ang).astype(dtype)


def _partial_rope(x, cos, sin):
    cos = cos[:, :, None, None, :]
    sin = sin[:, :, None, None, :]
    half = ROT_DIM // 2
    x1 = x[..., :half]
    x2 = x[..., half:ROT_DIM]
    return jnp.concatenate([x1 * cos - x2 * sin, x2 * cos + x1 * sin, x[..., ROT_DIM:]], axis=-1)


def _to_blocks(t):
    b, s = t.shape[:2]
    t = t.reshape((b, s // BLOCK_Q, BLOCK_Q) + t.shape[2:])
    return jnp.moveaxis(t, 1, 0)


def _from_blocks(t):
    t = jnp.moveaxis(t, 0, 1)
    return t.reshape((t.shape[0], t.shape[1] * t.shape[2]) + t.shape[3:])


def _causal_mask(blk, s):
    q_idx = blk * BLOCK_Q + jnp.arange(BLOCK_Q)
    return jnp.arange(s)[None, :] <= q_idx[:, None]


def _diff_attention(hn, w_qkv, lam_params, subln_gain, w_o, cos, sin, layer_idx):
    b, s, _ = hn.shape
    q, k, v = jnp.split(hn @ w_qkv, 3, axis=-1)
    q = _partial_rope(q.reshape(b, s, DIFF_HEADS, 2, DIFF_HEAD_DIM), cos, sin)
    k = _partial_rope(k.reshape(b, s, DIFF_HEADS, 2, DIFF_HEAD_DIM), cos, sin)
    v = v.reshape(b, s, DIFF_HEADS, 2 * DIFF_HEAD_DIM)
    lam_init = 0.8 - 0.6 * math.exp(-0.3 * layer_idx)
    lp = lam_params.astype(jnp.float32)
    lam = jnp.exp(jnp.sum(lp[0] * lp[1])) - jnp.exp(jnp.sum(lp[2] * lp[3])) + lam_init
    scale = DIFF_HEAD_DIM ** -0.5

    def block(args):
        qb, blk = args
        sc = jnp.einsum('bqhcd,bkhcd->bhcqk', qb, k).astype(jnp.float32) * scale
        sc = jnp.where(_causal_mask(blk, s), sc, -jnp.inf)
        pr = jax.nn.softmax(sc, axis=-1)
        a = pr[:, :, 0] - lam * pr[:, :, 1]
        return jnp.einsum('bhqk,bkhe->bqhe', a.astype(v.dtype), v)

    o = _from_blocks(lax.map(block, (_to_blocks(q), jnp.arange(s // BLOCK_Q))))
    o = _rms_norm(o, subln_gain) * (1.0 - lam_init)
    return o.reshape(b, s, -1) @ w_o


def _fox_shared_kv(h, kv_norm, kv_w, kv_b_f):
    b, s, _ = h.shape
    proj = _rms_norm(h, kv_norm) @ kv_w
    k = proj[..., :D_MODEL].reshape(b, s, FOX_HEADS, FOX_HEAD_DIM)
    v = proj[..., D_MODEL:2 * D_MODEL].reshape(b, s, FOX_HEADS, FOX_HEAD_DIM)
    f_logit = (proj[..., 2 * D_MODEL:] + kv_b_f).astype(jnp.float32)
    c = jnp.cumsum(jax.nn.log_sigmoid(f_logit), axis=1)
    return k, v, c


def _fox_attention(hn, w_q, w_o, k, v, c):
    b, s, _ = hn.shape
    q = (hn @ w_q).reshape(b, s, FOX_HEADS, FOX_HEAD_DIM)
    scale = FOX_HEAD_DIM ** -0.5
    c_k = jnp.moveaxis(c, 1, 2)

    def block(args):
        qb, cq, blk = args
        sc = jnp.einsum('bqhd,bkhd->bhqk', qb, k).astype(jnp.float32) * scale
        sc = sc + jnp.moveaxis(cq, 1, 2)[..., :, None] - c_k[:, :, None, :]
        sc = jnp.where(_causal_mask(blk, s), sc, -jnp.inf)
        pr = jax.nn.softmax(sc, axis=-1)
        return jnp.einsum('bhqk,bkhd->bqhd', pr.astype(v.dtype), v)

    o = _from_blocks(lax.map(block, (_to_blocks(q), _to_blocks(c), jnp.arange(s // BLOCK_Q))))
    return o.reshape(b, s, -1) @ w_o


def setup_inputs(seed: int = 0) -> dict:
    key = jax.random.key(seed)
    ks = jax.random.split(key, 24)

    def nrm(k, shape, scale):
        return jax.random.normal(k, shape, jnp.float32) * scale

    def gain(k, shape):
        return 1.0 + 0.01 * jax.random.normal(k, shape, jnp.float32)

    offsets = jax.random.randint(ks[2], (BATCH, 1), 0, 4096, dtype=jnp.int32)
    positions = (offsets + jnp.arange(SEQ, dtype=jnp.int32)[None, :]).astype(jnp.int32)
    return {
        "x": nrm(ks[0], (BATCH, SEQ, D_MODEL), 1.0),
        "p": nrm(ks[1], (DEPTH, BATCH, SEQ, PLE_DIM), 1.0),
        "positions": positions,
        "a_attn_norm": gain(ks[3], (N_A_LAYERS, D_MODEL)),
        "a_w_qkv": nrm(ks[4], (N_A_LAYERS, D_MODEL, 3 * D_MODEL), D_MODEL ** -0.5),
        "a_lambda": nrm(ks[5], (N_A_LAYERS, 4, DIFF_HEAD_DIM), 0.1),
        "a_subln": gain(ks[6], (N_A_LAYERS, 2 * DIFF_HEAD_DIM)),
        "a_w_o": nrm(ks[7], (N_A_LAYERS, D_MODEL, D_MODEL), D_MODEL ** -0.5),
        "kv_norm": gain(ks[8], (D_MODEL,)),
        "kv_w": nrm(ks[9], (D_MODEL, 2 * D_MODEL + FOX_HEADS), D_MODEL ** -0.5),
        "kv_b_f": jax.random.uniform(ks[10], (FOX_HEADS,), jnp.float32, 1.0, 5.0),
        "b_attn_norm": gain(ks[11], (N_B_LAYERS, D_MODEL)),
        "b_w_q": nrm(ks[12], (N_B_LAYERS, D_MODEL, D_MODEL), D_MODEL ** -0.5),
        "b_w_o": nrm(ks[13], (N_B_LAYERS, D_MODEL, D_MODEL), D_MODEL ** -0.5),
        "mlp_norm": gain(ks[14], (DEPTH, D_MODEL)),
        "mlp_w1": nrm(ks[15], (DEPTH, D_MODEL, D_FF), D_MODEL ** -0.5),
        "mlp_w2": nrm(ks[16], (DEPTH, D_FF, D_MODEL), D_FF ** -0.5),
        "ple_gate_norm": gain(ks[17], (DEPTH, D_MODEL)),
        "ple_gate_w": nrm(ks[18], (DEPTH, D_MODEL, D_MODEL), D_MODEL ** -0.5),
        "ple_w": nrm(ks[19], (DEPTH, PLE_DIM, D_MODEL), PLE_DIM ** -0.5),
        "final_norm": gain(ks[20], (D_MODEL,)),
    }


def reference(x, p, positions, a_attn_norm, a_w_qkv, a_lambda, a_subln, a_w_o, kv_norm, kv_w, kv_b_f,
              b_attn_norm, b_w_q, b_w_o, mlp_norm, mlp_w1, mlp_w2, ple_gate_norm, ple_gate_w, ple_w,
              final_norm):
    cos, sin = _rope_tables(positions, x.dtype)
    h = x
    for i in range(DEPTH):
        if i < N_A_LAYERS:
            h = h + _diff_attention(_rms_norm(h, a_attn_norm[i]), a_w_qkv[i], a_lambda[i], a_subln[i],
                                    a_w_o[i], cos, sin, i)
        else:
            j = i - N_A_LAYERS
            if j == 0:
                k_sh, v_sh, c_sh = _fox_shared_kv(h, kv_norm, kv_w, kv_b_f)
            h = h + _fox_attention(_rms_norm(h, b_attn_norm[j]), b_w_q[j], b_w_o[j], k_sh, v_sh, c_sh)
        hn = _rms_norm(h, mlp_norm[i])
        h = h + jnp.square(jax.nn.relu(hn @ mlp_w1[i])) @ mlp_w2[i]
        gate = jax.nn.sigmoid(_rms_norm(h, ple_gate_norm[i]) @ ple_gate_w[i])
        h = h + (p[i] @ ple_w[i]) * gate
    return _rms_norm(h, final_norm)
```

```python
import functools
import math

import jax
import jax.numpy as jnp
from jax import lax
from jax.experimental import pallas as pl
from jax.experimental.pallas import tpu as pltpu

D_MODEL = 1024
PLE_DIM = 256
HEAD_DIM = 64
LANES = 128
HEAD_PAIRS = D_MODEL // LANES
FOX_HEADS = D_MODEL // HEAD_DIM
D_FF = 4 * D_MODEL
ROT_DIM = HEAD_DIM // 4
ROT_HALF = ROT_DIM // 2
ROPE_THETA = 500000.0
RMS_EPS = 1e-6
ATTN_SCALE = HEAD_DIM ** -0.5

ROW_TILE = 512
ATTN_TQ = 512
ATTN_TK = 512
FF_CHUNK = 1024
VMEM_LIMIT = 56 * 1024 * 1024

F32 = jnp.float32
BF16 = jnp.bfloat16


def _rms(x, gain):
    return x * lax.rsqrt(jnp.mean(x * x, axis=-1, keepdims=True) + RMS_EPS) * gain


def _resident(shape):
    zeros = (0,) * len(shape)
    return pl.BlockSpec(shape, lambda *_: zeros, pipeline_mode=pl.Buffered(1))


def _params(*sem):
    return pltpu.CompilerParams(dimension_semantics=sem, vmem_limit_bytes=VMEM_LIMIT)


def _rope_table_kernel(pos_ref, invf_ref, c_ref, s1_ref, s2_ref):
    ang = pos_ref[...].astype(F32) * invf_ref[...]
    cos = jnp.cos(ang)
    sin = jnp.sin(ang)
    jj = lax.broadcasted_iota(jnp.int32, ang.shape, 1) % HEAD_DIM
    c_ref[...] = jnp.where(jj < ROT_DIM, cos, 1.0)
    s1_ref[...] = jnp.where(jj < ROT_HALF, -sin, 0.0)
    s2_ref[...] = jnp.where(jj < ROT_HALF, 0.0, jnp.where(jj < ROT_DIM, sin, 0.0))


def _rope_tables(positions):
    n = positions.size
    tm = min(ROW_TILE, n)
    inv_freq = 1.0 / (ROPE_THETA ** (jnp.arange(0, ROT_DIM, 2, dtype=F32) / ROT_DIM))
    jj = jnp.arange(LANES) % HEAD_DIM
    invf = jnp.where(jj < ROT_DIM, inv_freq[jj % ROT_HALF], 0.0).reshape(1, LANES)
    row = pl.BlockSpec((tm, LANES), lambda i: (i, 0))
    return pl.pallas_call(
        _rope_table_kernel,
        grid=(n // tm,),
        in_specs=[pl.BlockSpec((tm, 1), lambda i: (i, 0)), pl.BlockSpec((1, LANES), lambda i: (0, 0))],
        out_specs=[row, row, row],
        out_shape=[jax.ShapeDtypeStruct((n, LANES), F32)] * 3,
        compiler_params=_params("parallel"),
        name="rope_tables",
    )(positions.reshape(n, 1), invf)


def _rope_store(y, c, s1, s2, scale, out_ref):
    for j in range(y.shape[1] // LANES):
        yc = y[:, j * LANES:(j + 1) * LANES]
        r = yc * c + pltpu.roll(yc, LANES - ROT_HALF, 1) * s1 + pltpu.roll(yc, ROT_HALF, 1) * s2
        out_ref[:, j * LANES:(j + 1) * LANES] = (r * scale).astype(out_ref.dtype)


def _diff_proj_kernel(h_ref, g_ref, w_ref, c_ref, s1_ref, s2_ref, q_ref, k_ref, v_ref):
    hn = _rms(h_ref[...], g_ref[...]).astype(BF16)
    c, s1, s2 = c_ref[...], s1_ref[...], s2_ref[...]
    q = jnp.dot(hn, w_ref[:, 0:D_MODEL], preferred_element_type=F32)
    _rope_store(q, c, s1, s2, ATTN_SCALE, q_ref)
    k = jnp.dot(hn, w_ref[:, D_MODEL:2 * D_MODEL], preferred_element_type=F32)
    _rope_store(k, c, s1, s2, 1.0, k_ref)
    v = jnp.dot(hn, w_ref[:, 2 * D_MODEL:3 * D_MODEL], preferred_element_type=F32)
    v_ref[...] = v.astype(v_ref.dtype)


def _diff_proj(h, gain, w_qkv, tabs):
    n = h.shape[0]
    tm = min(ROW_TILE, n)
    row = pl.BlockSpec((tm, D_MODEL), lambda i: (i, 0))
    tab = pl.BlockSpec((tm, LANES), lambda i: (i, 0))
    return pl.pallas_call(
        _diff_proj_kernel,
        grid=(n // tm,),
        in_specs=[row, _resident((1, D_MODEL)), _resident((D_MODEL, 3 * D_MODEL)), tab, tab, tab],
        out_specs=[row, row, row],
        out_shape=[jax.ShapeDtypeStruct((n, D_MODEL), BF16)] * 3,
        compiler_params=_params("parallel"),
        name="diff_proj",
    )(h, gain.reshape(1, D_MODEL), w_qkv, *tabs)


def _fox_q_kernel(h_ref, g_ref, w_ref, q_ref):
    hn = _rms(h_ref[...], g_ref[...]).astype(BF16)
    q = jnp.dot(hn, w_ref[...], preferred_element_type=F32)
    q_ref[...] = (q * ATTN_SCALE).astype(q_ref.dtype)


def _fox_q(h, gain, w_q):
    n = h.shape[0]
    tm = min(ROW_TILE, n)
    row = pl.BlockSpec((tm, D_MODEL), lambda i: (i, 0))
    return pl.pallas_call(
        _fox_q_kernel,
        grid=(n // tm,),
        in_specs=[row, _resident((1, D_MODEL)), _resident((D_MODEL, D_MODEL))],
        out_specs=row,
        out_shape=jax.ShapeDtypeStruct((n, D_MODEL), BF16),
        compiler_params=_params("parallel"),
        name="fox_q",
    )(h, gain.reshape(1, D_MODEL), w_q)


def _fox_kv_kernel(h_ref, g_ref, w_ref, wft_ref, k_ref, v_ref, ft_ref):
    hn = _rms(h_ref[...], g_ref[...]).astype(BF16)
    k_ref[...] = jnp.dot(hn, w_ref[:, 0:D_MODEL], preferred_element_type=F32).astype(k_ref.dtype)
    v_ref[...] = jnp.dot(hn, w_ref[:, D_MODEL:2 * D_MODEL], preferred_element_type=F32).astype(v_ref.dtype)
    ft_ref[...] = lax.dot_general(wft_ref[...], hn, (((1,), (1,)), ((), ())), preferred_element_type=F32)


def _fox_kv(h, gain, w_kv, w_f_t):
    n = h.shape[0]
    tm = min(ROW_TILE, n)
    row = pl.BlockSpec((tm, D_MODEL), lambda i: (i, 0))
    return pl.pallas_call(
        _fox_kv_kernel,
        grid=(n // tm,),
        in_specs=[row, _resident((1, D_MODEL)), _resident((D_MODEL, 2 * D_MODEL)), _resident((FOX_HEADS, D_MODEL))],
        out_specs=[row, row, pl.BlockSpec((FOX_HEADS, tm), lambda i: (0, i))],
        out_shape=[jax.ShapeDtypeStruct((n, D_MODEL), BF16)] * 2 + [jax.ShapeDtypeStruct((FOX_HEADS, n), F32)],
        compiler_params=_params("parallel"),
        name="fox_kv",
    )(h, gain.reshape(1, D_MODEL), w_kv, w_f_t)


def _forget_cumsum_kernel(f_ref, b_ref, c_ref):
    x = jax.nn.log_sigmoid(f_ref[...] + b_ref[...])
    s = x.shape[1]
    lane = lax.broadcasted_iota(jnp.int32, x.shape, 1)
    shift = 1
    while shift < s:
        x = x + jnp.where(lane >= shift, pltpu.roll(x, shift, 1), 0.0)
        shift *= 2
    c_ref[...] = x


def _forget_cumsum(f_t, bias, batch):
    seq = f_t.shape[1] // batch
    return pl.pallas_call(
        _forget_cumsum_kernel,
        grid=(batch,),
        in_specs=[pl.BlockSpec((FOX_HEADS, seq), lambda b: (0, b)), pl.BlockSpec((FOX_HEADS, 1), lambda b: (0, 0))],
        out_specs=pl.BlockSpec((None, FOX_HEADS, seq), lambda b: (b, 0, 0)),
        out_shape=jax.ShapeDtypeStruct((batch, FOX_HEADS, seq), F32),
        compiler_params=_params("parallel"),
        name="forget_cumsum",
    )(f_t, bias.reshape(FOX_HEADS, 1))


def _attn_kernel(*refs, fox, lam_init, tk):
    if fox:
        q_ref, k_ref, v_ref, ccol_ref, crow_ref, o_ref, qm_ref, m_ref, l_ref, acc_ref = refs
    else:
        q_ref, k_ref, v_ref, lam_ref, sub_ref, o_ref, qm_ref, m_ref, l_ref, acc_ref = refs
    tq = q_ref.shape[0]
    qi = pl.program_id(2)

    lane = lax.broadcasted_iota(jnp.int32, (tq, LANES), 1)
    low = lane < HEAD_DIM
    q = q_ref[...]
    zero = jnp.zeros_like(q)
    qm_ref[0] = jnp.where(low, q, zero)
    qm_ref[1] = jnp.where(low, zero, q)
    m_ref[...] = jnp.full(m_ref.shape, -jnp.inf, F32)
    l_ref[...] = jnp.zeros(l_ref.shape, F32)
    acc_ref[...] = jnp.zeros(acc_ref.shape, F32)

    def chunk(j, masked):
        start = pl.multiple_of(j * tk, tk)
        k = k_ref[pl.ds(start, tk), :]
        v = v_ref[pl.ds(start, tk), :]
        if masked:
            row = lax.broadcasted_iota(jnp.int32, (tq, tk), 0)
            col = lax.broadcasted_iota(jnp.int32, (tq, tk), 1)
            visible = (start + col) <= (qi * tq + row)
        for c in range(2):
            s = lax.dot_general(qm_ref[c], k, (((1,), (1,)), ((), ())), preferred_element_type=F32)
            if fox:
                s = s + (ccol_ref[:, c:c + 1] - crow_ref[c:c + 1, pl.ds(start, tk)])
            if masked:
                s = jnp.where(visible, s, -jnp.inf)
            m_old = m_ref[c]
            m_new = jnp.maximum(m_old, jnp.max(s, axis=-1, keepdims=True))
            alpha = jnp.exp(m_old - m_new)
            p = jnp.exp(s - m_new)
            l_ref[c] = alpha * l_ref[c] + jnp.sum(p, axis=-1, keepdims=True)
            acc_ref[c] = alpha * acc_ref[c] + jnp.dot(p.astype(BF16), v, preferred_element_type=F32)
            m_ref[c] = m_new

    n_full = (qi * tq) // tk

    def body(j, carry):
        chunk(j, masked=False)
        return carry

    lax.fori_loop(0, n_full, body, 0)
    for d in range(tq // tk):
        chunk(n_full + d, masked=True)

    o0 = acc_ref[0] / l_ref[0]
    o1 = acc_ref[1] / l_ref[1]
    if fox:
        o_ref[...] = jnp.where(low, o0, o1).astype(o_ref.dtype)
    else:
        lp = lam_ref[...]
        lam = (jnp.exp(jnp.sum(lp[0:1] * lp[1:2], axis=-1, keepdims=True))
               - jnp.exp(jnp.sum(lp[2:3] * lp[3:4], axis=-1, keepdims=True)) + lam_init)
        o = o0 - lam * o1
        o_ref[...] = (_rms(o, sub_ref[...]) * (1.0 - lam_init)).astype(o_ref.dtype)


def _attention(q, k, v, extras, *, fox, lam_init=0.0):
    b, s, _ = q.shape
    tq = min(ATTN_TQ, s)
    tk = min(ATTN_TK, tq)
    qspec = pl.BlockSpec((None, tq, LANES), lambda bi, hp, qi: (bi, qi, hp))
    kvspec = pl.BlockSpec((None, s, LANES), lambda bi, hp, qi: (bi, 0, hp))
    if fox:
        extra_specs = [pl.BlockSpec((None, None, tq, 2), lambda bi, hp, qi: (bi, hp, qi, 0)),
                       pl.BlockSpec((None, None, 2, s), lambda bi, hp, qi: (bi, hp, 0, 0))]
    else:
        extra_specs = [pl.BlockSpec((4, HEAD_DIM), lambda bi, hp, qi: (0, 0)),
                       pl.BlockSpec((1, LANES), lambda bi, hp, qi: (0, 0))]
    return pl.pallas_call(
        functools.partial(_attn_kernel, fox=fox, lam_init=lam_init, tk=tk),
        grid=(b, HEAD_PAIRS, s // tq),
        in_specs=[qspec, kvspec, kvspec] + extra_specs,
        out_specs=qspec,
        out_shape=jax.ShapeDtypeStruct((b, s, D_MODEL), BF16),
        scratch_shapes=[pltpu.VMEM((2, tq, LANES), BF16),
                        pltpu.VMEM((2, tq, 1), F32),
                        pltpu.VMEM((2, tq, 1), F32),
                        pltpu.VMEM((2, tq, LANES), F32)],
        compiler_params=_params("parallel", "parallel", "arbitrary"),
        name="fox_attn" if fox else "diff_attn",
    )(q, k, v, *extras)


def _post_kernel(h_ref, o_ref, p_ref, wo_ref, gm_ref, w1_ref, w2_ref, gp_ref, wg_ref, wp_ref, gf_ref,
                 out_ref, *, final):
    h = h_ref[...] + jnp.dot(o_ref[...], wo_ref[...], preferred_element_type=F32)
    hn = _rms(h, gm_ref[...]).astype(BF16)
    for c in range(D_FF // FF_CHUNK):
        u = jnp.dot(hn, w1_ref[:, c * FF_CHUNK:(c + 1) * FF_CHUNK], preferred_element_type=F32)
        a = jnp.square(jnp.maximum(u, 0.0)).astype(BF16)
        h = h + jnp.dot(a, w2_ref[c * FF_CHUNK:(c + 1) * FF_CHUNK, :], preferred_element_type=F32)
    gate = jax.nn.sigmoid(jnp.dot(_rms(h, gp_ref[...]).astype(BF16), wg_ref[...], preferred_element_type=F32))
    emb = jnp.dot(p_ref[...].astype(BF16), wp_ref[...], preferred_element_type=F32)
    h = h + emb * gate
    if final:
        h = _rms(h, gf_ref[...])
    out_ref[...] = h


def _post(h, o, p, w_o, g_mlp, w1, w2, g_ple, w_gate, w_ple, g_final, *, final):
    n = h.shape[0]
    tm = min(ROW_TILE, n)
    row = pl.BlockSpec((tm, D_MODEL), lambda i: (i, 0))
    gain = _resident((1, D_MODEL))
    return pl.pallas_call(
        functools.partial(_post_kernel, final=final),
        grid=(n // tm,),
        in_specs=[row, row, pl.BlockSpec((tm, PLE_DIM), lambda i: (i, 0)),
                  _resident((D_MODEL, D_MODEL)), gain, _resident((D_MODEL, D_FF)), _resident((D_FF, D_MODEL)),
                  gain, _resident((D_MODEL, D_MODEL)), _resident((PLE_DIM, D_MODEL)), gain],
        out_specs=row,
        out_shape=jax.ShapeDtypeStruct((n, D_MODEL), F32),
        compiler_params=_params("parallel"),
        name="post_block",
    )(h, o, p, w_o, g_mlp.reshape(1, D_MODEL), w1, w2, g_ple.reshape(1, D_MODEL), w_gate, w_ple,
      g_final.reshape(1, D_MODEL))


def kernel(x, p, positions, a_attn_norm, a_w_qkv, a_lambda, a_subln, a_w_o, kv_norm, kv_w, kv_b_f,
           b_attn_norm, b_w_q, b_w_o, mlp_norm, mlp_w1, mlp_w2, ple_gate_norm, ple_gate_w, ple_w,
           final_norm):
    b, s, _ = x.shape
    depth = p.shape[0]
    n_a = a_w_qkv.shape[0]
    n = b * s
    bf = lambda w: w.astype(BF16)

    tabs = _rope_tables(positions)
    h = x.reshape(n, D_MODEL)
    k_sh = v_sh = c_col = c_row = None
    for i in range(depth):
        if i < n_a:
            q, k, v = _diff_proj(h, a_attn_norm[i], bf(a_w_qkv[i]), tabs)
            lam_init = 0.8 - 0.6 * math.exp(-0.3 * i)
            o = _attention(q.reshape(b, s, D_MODEL), k.reshape(b, s, D_MODEL), v.reshape(b, s, D_MODEL),
                           (a_lambda[i], a_subln[i].reshape(1, LANES)), fox=False, lam_init=lam_init)
            w_o = a_w_o[i]
        else:
            j = i - n_a
            if j == 0:
                k_sh, v_sh, f_t = _fox_kv(h, kv_norm, bf(kv_w[:, :2 * D_MODEL]), bf(kv_w[:, 2 * D_MODEL:].T))
                c_t = _forget_cumsum(f_t, kv_b_f, b).reshape(b, HEAD_PAIRS, 2, s)
                c_row = c_t
                c_col = jnp.swapaxes(c_t, 2, 3)
                k_sh = k_sh.reshape(b, s, D_MODEL)
                v_sh = v_sh.reshape(b, s, D_MODEL)
            q = _fox_q(h, b_attn_norm[j], bf(b_w_q[j]))
            o = _attention(q.reshape(b, s, D_MODEL), k_sh, v_sh, (c_col, c_row), fox=True)
            w_o = b_w_o[j]
        h = _post(h, o.reshape(n, D_MODEL), p[i].reshape(n, PLE_DIM), bf(w_o), mlp_norm[i], bf(mlp_w1[i]),
                  bf(mlp_w2[i]), ple_gate_norm[i], bf(ple_gate_w[i]), bf(ple_w[i]), final_norm,
                  final=(i == depth - 1))
    return h.reshape(b, s, D_MODEL)
```

```python
import functools
import math

import jax
import jax.numpy as jnp
from jax import lax
from jax.experimental import pallas as pl
from jax.experimental.pallas import tpu as pltpu

D_MODEL = 1024
PLE_DIM = 256
HEAD_DIM = 64
LANES = 128
HEAD_PAIRS = D_MODEL // LANES
FOX_HEADS = D_MODEL // HEAD_DIM
D_FF = 4 * D_MODEL
ROT_DIM = HEAD_DIM // 4
ROT_HALF = ROT_DIM // 2
ROPE_THETA = 500000.0
RMS_EPS = 1e-6
LOG2E = math.log2(math.e)
ATTN_SCALE = HEAD_DIM ** -0.5 * LOG2E

ROW_TILE = 512
ATTN_TQ = 512
ATTN_TK = 512
FF_CHUNK = 1024
VMEM_LIMIT = 56 * 1024 * 1024

F32 = jnp.float32
BF16 = jnp.bfloat16


def _rms(x, gain):
    return x * lax.rsqrt(jnp.mean(x * x, axis=-1, keepdims=True) + RMS_EPS) * gain


def _resident(shape):
    zeros = (0,) * len(shape)
    return pl.BlockSpec(shape, lambda *_: zeros, pipeline_mode=pl.Buffered(1))


def _params(*sem):
    return pltpu.CompilerParams(dimension_semantics=sem, vmem_limit_bytes=VMEM_LIMIT)


def _rope_table_kernel(pos_ref, invf_ref, c_ref, s1_ref, s2_ref):
    ang = pos_ref[...].astype(F32) * invf_ref[...]
    cos = jnp.cos(ang)
    sin = jnp.sin(ang)
    jj = lax.broadcasted_iota(jnp.int32, ang.shape, 1) % HEAD_DIM
    c_ref[...] = jnp.where(jj < ROT_DIM, cos, 1.0)
    s1_ref[...] = jnp.where(jj < ROT_HALF, -sin, 0.0)
    s2_ref[...] = jnp.where(jj < ROT_HALF, 0.0, jnp.where(jj < ROT_DIM, sin, 0.0))


def _rope_tables(positions):
    n = positions.size
    tm = min(ROW_TILE, n)
    inv_freq = 1.0 / (ROPE_THETA ** (jnp.arange(0, ROT_DIM, 2, dtype=F32) / ROT_DIM))
    jj = jnp.arange(LANES) % HEAD_DIM
    invf = jnp.where(jj < ROT_DIM, inv_freq[jj % ROT_HALF], 0.0).reshape(1, LANES)
    row = pl.BlockSpec((tm, LANES), lambda i: (i, 0))
    return pl.pallas_call(
        _rope_table_kernel,
        grid=(n // tm,),
        in_specs=[pl.BlockSpec((tm, 1), lambda i: (i, 0)), pl.BlockSpec((1, LANES), lambda i: (0, 0))],
        out_specs=[row, row, row],
        out_shape=[jax.ShapeDtypeStruct((n, LANES), F32)] * 3,
        compiler_params=_params("parallel"),
        name="rope_tables",
    )(positions.reshape(n, 1), invf)


def _rope_store(y, c, s1, s2, scale, out_ref):
    for j in range(y.shape[1] // LANES):
        yc = y[:, j * LANES:(j + 1) * LANES]
        r = yc * c + pltpu.roll(yc, LANES - ROT_HALF, 1) * s1 + pltpu.roll(yc, ROT_HALF, 1) * s2
        out_ref[:, j * LANES:(j + 1) * LANES] = (r * scale).astype(out_ref.dtype)


def _diff_proj_kernel(h_ref, g_ref, w_ref, c_ref, s1_ref, s2_ref, q_ref, k_ref, v_ref):
    hn = _rms(h_ref[...], g_ref[...]).astype(BF16)
    c, s1, s2 = c_ref[...], s1_ref[...], s2_ref[...]
    q = jnp.dot(hn, w_ref[:, 0:D_MODEL], preferred_element_type=F32)
    _rope_store(q, c, s1, s2, ATTN_SCALE, q_ref)
    k = jnp.dot(hn, w_ref[:, D_MODEL:2 * D_MODEL], preferred_element_type=F32)
    _rope_store(k, c, s1, s2, 1.0, k_ref)
    v = jnp.dot(hn, w_ref[:, 2 * D_MODEL:3 * D_MODEL], preferred_element_type=F32)
    v_ref[...] = v.astype(v_ref.dtype)


def _diff_proj(h, gain, w_qkv, tabs):
    n = h.shape[0]
    tm = min(ROW_TILE, n)
    row = pl.BlockSpec((tm, D_MODEL), lambda i: (i, 0))
    tab = pl.BlockSpec((tm, LANES), lambda i: (i, 0))
    return pl.pallas_call(
        _diff_proj_kernel,
        grid=(n // tm,),
        in_specs=[row, _resident((1, D_MODEL)), _resident((D_MODEL, 3 * D_MODEL)), tab, tab, tab],
        out_specs=[row, row, row],
        out_shape=[jax.ShapeDtypeStruct((n, D_MODEL), BF16)] * 3,
        compiler_params=_params("parallel"),
        name="diff_proj",
    )(h, gain.reshape(1, D_MODEL), w_qkv, *tabs)


def _fox_q_kernel(h_ref, g_ref, w_ref, q_ref):
    hn = _rms(h_ref[...], g_ref[...]).astype(BF16)
    q = jnp.dot(hn, w_ref[...], preferred_element_type=F32)
    q_ref[...] = (q * ATTN_SCALE).astype(q_ref.dtype)


def _fox_q(h, gain, w_q):
    n = h.shape[0]
    tm = min(ROW_TILE, n)
    row = pl.BlockSpec((tm, D_MODEL), lambda i: (i, 0))
    return pl.pallas_call(
        _fox_q_kernel,
        grid=(n // tm,),
        in_specs=[row, _resident((1, D_MODEL)), _resident((D_MODEL, D_MODEL))],
        out_specs=row,
        out_shape=jax.ShapeDtypeStruct((n, D_MODEL), BF16),
        compiler_params=_params("parallel"),
        name="fox_q",
    )(h, gain.reshape(1, D_MODEL), w_q)


def _fox_kv_kernel(h_ref, g_ref, w_ref, wft_ref, k_ref, v_ref, ft_ref):
    hn = _rms(h_ref[...], g_ref[...]).astype(BF16)
    k_ref[...] = jnp.dot(hn, w_ref[:, 0:D_MODEL], preferred_element_type=F32).astype(k_ref.dtype)
    v_ref[...] = jnp.dot(hn, w_ref[:, D_MODEL:2 * D_MODEL], preferred_element_type=F32).astype(v_ref.dtype)
    ft_ref[...] = lax.dot_general(wft_ref[...], hn, (((1,), (1,)), ((), ())), preferred_element_type=F32)


def _fox_kv(h, gain, w_kv, w_f_t):
    n = h.shape[0]
    tm = min(ROW_TILE, n)
    row = pl.BlockSpec((tm, D_MODEL), lambda i: (i, 0))
    return pl.pallas_call(
        _fox_kv_kernel,
        grid=(n // tm,),
        in_specs=[row, _resident((1, D_MODEL)), _resident((D_MODEL, 2 * D_MODEL)), _resident((FOX_HEADS, D_MODEL))],
        out_specs=[row, row, pl.BlockSpec((FOX_HEADS, tm), lambda i: (0, i))],
        out_shape=[jax.ShapeDtypeStruct((n, D_MODEL), BF16)] * 2 + [jax.ShapeDtypeStruct((FOX_HEADS, n), F32)],
        compiler_params=_params("parallel"),
        name="fox_kv",
    )(h, gain.reshape(1, D_MODEL), w_kv, w_f_t)


def _forget_cumsum_kernel(f_ref, b_ref, c_ref):
    x = jax.nn.log_sigmoid(f_ref[...] + b_ref[...])
    s = x.shape[1]
    lane = lax.broadcasted_iota(jnp.int32, x.shape, 1)
    shift = 1
    while shift < s:
        x = x + jnp.where(lane >= shift, pltpu.roll(x, shift, 1), 0.0)
        shift *= 2
    c_ref[...] = x * LOG2E


def _forget_cumsum(f_t, bias, batch):
    seq = f_t.shape[1] // batch
    return pl.pallas_call(
        _forget_cumsum_kernel,
        grid=(batch,),
        in_specs=[pl.BlockSpec((FOX_HEADS, seq), lambda b: (0, b)), pl.BlockSpec((FOX_HEADS, 1), lambda b: (0, 0))],
        out_specs=pl.BlockSpec((None, FOX_HEADS, seq), lambda b: (b, 0, 0)),
        out_shape=jax.ShapeDtypeStruct((batch, FOX_HEADS, seq), F32),
        compiler_params=_params("parallel"),
        name="forget_cumsum",
    )(f_t, bias.reshape(FOX_HEADS, 1))


def _attn_kernel(*refs, fox, lam_init, tk):
    if fox:
        q_ref, k_ref, v_ref, ccol_ref, crow_ref, o_ref, qm_ref, m_ref, l_ref, acc_ref = refs
    else:
        q_ref, k_ref, v_ref, lam_ref, sub_ref, o_ref, qm_ref, m_ref, l_ref, acc_ref = refs
    tq = q_ref.shape[0]
    qi = pl.program_id(2)

    lane = lax.broadcasted_iota(jnp.int32, (tq, LANES), 1)
    low = lane < HEAD_DIM
    q = q_ref[...]
    zero = jnp.zeros_like(q)
    qm_ref[0] = jnp.where(low, q, zero)
    qm_ref[1] = jnp.where(low, zero, q)
    m_ref[...] = jnp.full(m_ref.shape, -jnp.inf, F32)
    l_ref[...] = jnp.zeros(l_ref.shape, F32)
    acc_ref[...] = jnp.zeros(acc_ref.shape, F32)

    def chunk(j, masked):
        start = pl.multiple_of(j * tk, tk)
        k = k_ref[pl.ds(start, tk), :]
        v = v_ref[pl.ds(start, tk), :]
        if masked:
            row = lax.broadcasted_iota(jnp.int32, (tq, tk), 0)
            col = lax.broadcasted_iota(jnp.int32, (tq, tk), 1)
            visible = (start + col) <= (qi * tq + row)
        for c in range(2):
            s = lax.dot_general(qm_ref[c], k, (((1,), (1,)), ((), ())), preferred_element_type=F32)
            if fox:
                s = s + (ccol_ref[:, c:c + 1] - crow_ref[c:c + 1, pl.ds(start, tk)])
            if masked:
                s = jnp.where(visible, s, -jnp.inf)
            m_old = m_ref[c]
            m_new = jnp.maximum(m_old, jnp.max(s, axis=-1, keepdims=True))
            alpha = jnp.exp2(m_old - m_new)
            ps = [jnp.exp2(s[:, i * LANES:(i + 1) * LANES] - m_new) for i in range(tk // LANES)]
            l_ref[c] = alpha * l_ref[c] + functools.reduce(jnp.add, ps)
            p = jnp.concatenate(ps, axis=1).astype(BF16)
            acc_ref[c] = alpha * acc_ref[c] + jnp.dot(p, v, preferred_element_type=F32)
            m_ref[c] = m_new

    n_full = (qi * tq) // tk

    def body(j, carry):
        chunk(j, masked=False)
        return carry

    lax.fori_loop(0, n_full, body, 0)
    for d in range(tq // tk):
        chunk(n_full + d, masked=True)

    o0 = acc_ref[0] / jnp.sum(l_ref[0], axis=-1, keepdims=True)
    o1 = acc_ref[1] / jnp.sum(l_ref[1], axis=-1, keepdims=True)
    if fox:
        o_ref[...] = jnp.where(low, o0, o1).astype(o_ref.dtype)
    else:
        lp = lam_ref[...]
        lam = (jnp.exp(jnp.sum(lp[0:1] * lp[1:2], axis=-1, keepdims=True))
               - jnp.exp(jnp.sum(lp[2:3] * lp[3:4], axis=-1, keepdims=True)) + lam_init)
        o = o0 - lam * o1
        o_ref[...] = (_rms(o, sub_ref[...]) * (1.0 - lam_init)).astype(o_ref.dtype)


def _attention(q, k, v, extras, *, fox, lam_init=0.0):
    b, s, _ = q.shape
    tq = min(ATTN_TQ, s)
    tk = min(ATTN_TK, tq)
    qspec = pl.BlockSpec((None, tq, LANES), lambda bi, hp, qi: (bi, qi, hp))
    kvspec = pl.BlockSpec((None, s, LANES), lambda bi, hp, qi: (bi, 0, hp))
    if fox:
        extra_specs = [pl.BlockSpec((None, None, tq, 2), lambda bi, hp, qi: (bi, hp, qi, 0)),
                       pl.BlockSpec((None, None, 2, s), lambda bi, hp, qi: (bi, hp, 0, 0))]
    else:
        extra_specs = [pl.BlockSpec((4, HEAD_DIM), lambda bi, hp, qi: (0, 0)),
                       pl.BlockSpec((1, LANES), lambda bi, hp, qi: (0, 0))]
    return pl.pallas_call(
        functools.partial(_attn_kernel, fox=fox, lam_init=lam_init, tk=tk),
        grid=(b, HEAD_PAIRS, s // tq),
        in_specs=[qspec, kvspec, kvspec] + extra_specs,
        out_specs=qspec,
        out_shape=jax.ShapeDtypeStruct((b, s, D_MODEL), BF16),
        scratch_shapes=[pltpu.VMEM((2, tq, LANES), BF16),
                        pltpu.VMEM((2, tq, LANES), F32),
                        pltpu.VMEM((2, tq, LANES), F32),
                        pltpu.VMEM((2, tq, LANES), F32)],
        compiler_params=_params("parallel", "parallel", "arbitrary"),
        name="fox_attn" if fox else "diff_attn",
    )(q, k, v, *extras)


def _post_kernel(h_ref, o_ref, p_ref, wo_ref, gm_ref, w1_ref, w2_ref, gp_ref, wg_ref, wp_ref, gf_ref,
                 out_ref, *, final):
    h = h_ref[...] + jnp.dot(o_ref[...], wo_ref[...], preferred_element_type=F32)
    hn = _rms(h, gm_ref[...]).astype(BF16)
    for c in range(D_FF // FF_CHUNK):
        u = jnp.dot(hn, w1_ref[:, c * FF_CHUNK:(c + 1) * FF_CHUNK], preferred_element_type=F32)
        a = jnp.square(jnp.maximum(u, 0.0)).astype(BF16)
        h = h + jnp.dot(a, w2_ref[c * FF_CHUNK:(c + 1) * FF_CHUNK, :], preferred_element_type=F32)
    gate = jax.nn.sigmoid(jnp.dot(_rms(h, gp_ref[...]).astype(BF16), wg_ref[...], preferred_element_type=F32))
    emb = jnp.dot(p_ref[...].astype(BF16), wp_ref[...], preferred_element_type=F32)
    h = h + emb * gate
    if final:
        h = _rms(h, gf_ref[...])
    out_ref[...] = h


def _post(h, o, p, w_o, g_mlp, w1, w2, g_ple, w_gate, w_ple, g_final, *, final):
    n = h.shape[0]
    tm = min(ROW_TILE, n)
    row = pl.BlockSpec((tm, D_MODEL), lambda i: (i, 0))
    gain = _resident((1, D_MODEL))
    return pl.pallas_call(
        functools.partial(_post_kernel, final=final),
        grid=(n // tm,),
        in_specs=[row, row, pl.BlockSpec((tm, PLE_DIM), lambda i: (i, 0)),
                  _resident((D_MODEL, D_MODEL)), gain, _resident((D_MODEL, D_FF)), _resident((D_FF, D_MODEL)),
                  gain, _resident((D_MODEL, D_MODEL)), _resident((PLE_DIM, D_MODEL)), gain],
        out_specs=row,
        out_shape=jax.ShapeDtypeStruct((n, D_MODEL), F32),
        compiler_params=_params("parallel"),
        name="post_block",
    )(h, o, p, w_o, g_mlp.reshape(1, D_MODEL), w1, w2, g_ple.reshape(1, D_MODEL), w_gate, w_ple,
      g_final.reshape(1, D_MODEL))


def kernel(x, p, positions, a_attn_norm, a_w_qkv, a_lambda, a_subln, a_w_o, kv_norm, kv_w, kv_b_f,
           b_attn_norm, b_w_q, b_w_o, mlp_norm, mlp_w1, mlp_w2, ple_gate_norm, ple_gate_w, ple_w,
           final_norm):
    b, s, _ = x.shape
    depth = p.shape[0]
    n_a = a_w_qkv.shape[0]
    n = b * s
    bf = lambda w: w.astype(BF16)

    tabs = _rope_tables(positions)
    h = x.reshape(n, D_MODEL)
    k_sh = v_sh = c_col = c_row = None
    for i in range(depth):
        if i < n_a:
            q, k, v = _diff_proj(h, a_attn_norm[i], bf(a_w_qkv[i]), tabs)
            lam_init = 0.8 - 0.6 * math.exp(-0.3 * i)
            o = _attention(q.reshape(b, s, D_MODEL), k.reshape(b, s, D_MODEL), v.reshape(b, s, D_MODEL),
                           (a_lambda[i], a_subln[i].reshape(1, LANES)), fox=False, lam_init=lam_init)
            w_o = a_w_o[i]
        else:
            j = i - n_a
            if j == 0:
                k_sh, v_sh, f_t = _fox_kv(h, kv_norm, bf(kv_w[:, :2 * D_MODEL]), bf(kv_w[:, 2 * D_MODEL:].T))
                c_t = _forget_cumsum(f_t, kv_b_f, b).reshape(b, HEAD_PAIRS, 2, s)
                c_row = c_t
                c_col = jnp.swapaxes(c_t, 2, 3)
                k_sh = k_sh.reshape(b, s, D_MODEL)
                v_sh = v_sh.reshape(b, s, D_MODEL)
            q = _fox_q(h, b_attn_norm[j], bf(b_w_q[j]))
            o = _attention(q.reshape(b, s, D_MODEL), k_sh, v_sh, (c_col, c_row), fox=True)
            w_o = b_w_o[j]
        h = _post(h, o.reshape(n, D_MODEL), p[i].reshape(n, PLE_DIM), bf(w_o), mlp_norm[i], bf(mlp_w1[i]),
                  bf(mlp_w2[i]), ple_gate_norm[i], bf(ple_gate_w[i]), bf(ple_w[i]), final_norm,
                  final=(i == depth - 1))
    return h.reshape(b, s, D_MODEL)
```

```python
import functools
import math

import jax
import jax.numpy as jnp
from jax import lax
from jax.experimental import pallas as pl
from jax.experimental.pallas import tpu as pltpu

D_MODEL = 1024
PLE_DIM = 256
HEAD_DIM = 64
LANES = 128
SUBLANES = 8
HEAD_PAIRS = D_MODEL // LANES
FOX_HEADS = D_MODEL // HEAD_DIM
D_FF = 4 * D_MODEL
ROT_DIM = HEAD_DIM // 4
ROT_HALF = ROT_DIM // 2
ROPE_THETA = 500000.0
RMS_EPS = 1e-6
LOG2E = math.log2(math.e)
ATTN_SCALE = HEAD_DIM ** -0.5 * LOG2E

ROW_TILE = 512
ATTN_TILE = 512
FF_CHUNK = 1024
VMEM_LIMIT = 56 * 1024 * 1024

F32 = jnp.float32
BF16 = jnp.bfloat16


def _rms(x, gain):
    return x * lax.rsqrt(jnp.mean(x * x, axis=-1, keepdims=True) + RMS_EPS) * gain


def _resident(shape):
    zeros = (0,) * len(shape)
    return pl.BlockSpec(shape, lambda *_: zeros, pipeline_mode=pl.Buffered(1))


def _params(*sem):
    return pltpu.CompilerParams(dimension_semantics=sem, vmem_limit_bytes=VMEM_LIMIT)


def _rope_table_kernel(pos_ref, invf_ref, c_ref, s1_ref, s2_ref):
    ang = pos_ref[...].astype(F32) * invf_ref[...]
    cos = jnp.cos(ang)
    sin = jnp.sin(ang)
    jj = lax.broadcasted_iota(jnp.int32, ang.shape, 1) % HEAD_DIM
    c_ref[...] = jnp.where(jj < ROT_DIM, cos, 1.0)
    s1_ref[...] = jnp.where(jj < ROT_HALF, -sin, 0.0)
    s2_ref[...] = jnp.where(jj < ROT_HALF, 0.0, jnp.where(jj < ROT_DIM, sin, 0.0))


def _rope_tables(positions):
    n = positions.size
    tm = min(ROW_TILE, n)
    inv_freq = 1.0 / (ROPE_THETA ** (jnp.arange(0, ROT_DIM, 2, dtype=F32) / ROT_DIM))
    jj = jnp.arange(LANES) % HEAD_DIM
    invf = jnp.where(jj < ROT_DIM, inv_freq[jj % ROT_HALF], 0.0).reshape(1, LANES)
    row = pl.BlockSpec((tm, LANES), lambda i: (i, 0))
    return pl.pallas_call(
        _rope_table_kernel,
        grid=(n // tm,),
        in_specs=[pl.BlockSpec((tm, 1), lambda i: (i, 0)), pl.BlockSpec((1, LANES), lambda i: (0, 0))],
        out_specs=[row, row, row],
        out_shape=[jax.ShapeDtypeStruct((n, LANES), F32)] * 3,
        compiler_params=_params("parallel"),
        name="rope_tables",
    )(positions.reshape(n, 1), invf)


def _rope_store(y, c, s1, s2, scale, out_ref):
    for j in range(y.shape[1] // LANES):
        yc = y[:, j * LANES:(j + 1) * LANES]
        r = yc * c + pltpu.roll(yc, LANES - ROT_HALF, 1) * s1 + pltpu.roll(yc, ROT_HALF, 1) * s2
        out_ref[:, j * LANES:(j + 1) * LANES] = (r * scale).astype(out_ref.dtype)


def _diff_proj_kernel(h_ref, g_ref, w_ref, c_ref, s1_ref, s2_ref, q_ref, k_ref, v_ref):
    hn = _rms(h_ref[...], g_ref[...]).astype(BF16)
    c, s1, s2 = c_ref[...], s1_ref[...], s2_ref[...]
    q = jnp.dot(hn, w_ref[:, 0:D_MODEL], preferred_element_type=F32)
    _rope_store(q, c, s1, s2, ATTN_SCALE, q_ref)
    k = jnp.dot(hn, w_ref[:, D_MODEL:2 * D_MODEL], preferred_element_type=F32)
    _rope_store(k, c, s1, s2, 1.0, k_ref)
    v = jnp.dot(hn, w_ref[:, 2 * D_MODEL:3 * D_MODEL], preferred_element_type=F32)
    v_ref[...] = v.astype(v_ref.dtype)


def _diff_proj(h, gain, w_qkv, tabs):
    n = h.shape[0]
    tm = min(ROW_TILE, n)
    row = pl.BlockSpec((tm, D_MODEL), lambda i: (i, 0))
    tab = pl.BlockSpec((tm, LANES), lambda i: (i, 0))
    return pl.pallas_call(
        _diff_proj_kernel,
        grid=(n // tm,),
        in_specs=[row, _resident((1, D_MODEL)), _resident((D_MODEL, 3 * D_MODEL)), tab, tab, tab],
        out_specs=[row, row, row],
        out_shape=[jax.ShapeDtypeStruct((n, D_MODEL), BF16)] * 3,
        compiler_params=_params("parallel"),
        name="diff_proj",
    )(h, gain.reshape(1, D_MODEL), w_qkv, *tabs)


def _fox_q_kernel(h_ref, g_ref, w_ref, q_ref):
    hn = _rms(h_ref[...], g_ref[...]).astype(BF16)
    q = jnp.dot(hn, w_ref[...], preferred_element_type=F32)
    q_ref[...] = (q * ATTN_SCALE).astype(q_ref.dtype)


def _fox_q(h, gain, w_q):
    n = h.shape[0]
    tm = min(ROW_TILE, n)
    row = pl.BlockSpec((tm, D_MODEL), lambda i: (i, 0))
    return pl.pallas_call(
        _fox_q_kernel,
        grid=(n // tm,),
        in_specs=[row, _resident((1, D_MODEL)), _resident((D_MODEL, D_MODEL))],
        out_specs=row,
        out_shape=jax.ShapeDtypeStruct((n, D_MODEL), BF16),
        compiler_params=_params("parallel"),
        name="fox_q",
    )(h, gain.reshape(1, D_MODEL), w_q)


def _fox_kv_kernel(h_ref, g_ref, w_ref, wft_ref, k_ref, v_ref, ft_ref):
    hn = _rms(h_ref[...], g_ref[...]).astype(BF16)
    k_ref[...] = jnp.dot(hn, w_ref[:, 0:D_MODEL], preferred_element_type=F32).astype(k_ref.dtype)
    v_ref[...] = jnp.dot(hn, w_ref[:, D_MODEL:2 * D_MODEL], preferred_element_type=F32).astype(v_ref.dtype)
    ft_ref[...] = lax.dot_general(wft_ref[...], hn, (((1,), (1,)), ((), ())), preferred_element_type=F32)


def _fox_kv(h, gain, w_kv, w_f_t):
    n = h.shape[0]
    tm = min(ROW_TILE, n)
    row = pl.BlockSpec((tm, D_MODEL), lambda i: (i, 0))
    return pl.pallas_call(
        _fox_kv_kernel,
        grid=(n // tm,),
        in_specs=[row, _resident((1, D_MODEL)), _resident((D_MODEL, 2 * D_MODEL)), _resident((FOX_HEADS, D_MODEL))],
        out_specs=[row, row, pl.BlockSpec((FOX_HEADS, tm), lambda i: (0, i))],
        out_shape=[jax.ShapeDtypeStruct((n, D_MODEL), BF16)] * 2 + [jax.ShapeDtypeStruct((FOX_HEADS, n), F32)],
        compiler_params=_params("parallel"),
        name="fox_kv",
    )(h, gain.reshape(1, D_MODEL), w_kv, w_f_t)


def _forget_cumsum_kernel(f_ref, b_ref, c_ref):
    x = jax.nn.log_sigmoid(f_ref[...] + b_ref[...])
    s = x.shape[1]
    lane = lax.broadcasted_iota(jnp.int32, x.shape, 1)
    shift = 1
    while shift < s:
        x = x + jnp.where(lane >= shift, pltpu.roll(x, shift, 1), 0.0)
        shift *= 2
    c_ref[...] = x * LOG2E


def _forget_cumsum(f_t, bias, batch):
    seq = f_t.shape[1] // batch
    return pl.pallas_call(
        _forget_cumsum_kernel,
        grid=(batch,),
        in_specs=[pl.BlockSpec((FOX_HEADS, seq), lambda b: (0, b)), pl.BlockSpec((FOX_HEADS, 1), lambda b: (0, 0))],
        out_specs=pl.BlockSpec((None, FOX_HEADS, seq), lambda b: (b, 0, 0)),
        out_shape=jax.ShapeDtypeStruct((batch, FOX_HEADS, seq), F32),
        compiler_params=_params("parallel"),
        name="forget_cumsum",
    )(f_t, bias.reshape(FOX_HEADS, 1))


def _attn_kernel(*refs, fox, lam_init, tile, pairs, n_lower):
    if fox:
        (qt_ref, kt_ref, q_ref, k_ref, vt_ref, cq_ref, ck_ref, o_ref,
         qm_ref, m_ref, l_ref, acc_ref, s_ref, mc_ref, p_ref, alpha_ref) = refs
    else:
        (qt_ref, kt_ref, q_ref, k_ref, vt_ref, lam_ref, sub_ref, o_ref,
         qm_ref, m_ref, l_ref, acc_ref, s_ref, mc_ref, p_ref, alpha_ref) = refs
    n_tiles = q_ref.shape[0] // tile
    n_pairs = len(pairs)

    lane = lax.broadcasted_iota(jnp.int32, (tile, LANES), 1)

    def init_tile(i, carry):
        q = q_ref[pl.ds(pl.multiple_of(i * tile, tile), tile), :]
        zero = jnp.zeros_like(q)
        qm_ref[0, i] = jnp.where(lane < HEAD_DIM, q, zero)
        qm_ref[1, i] = jnp.where(lane < HEAD_DIM, zero, q)
        for c in range(2):
            m_ref[c, i] = jnp.full((SUBLANES, tile), -jnp.inf, F32)
            l_ref[c, i] = jnp.zeros((SUBLANES, tile), F32)
            acc_ref[c, i] = jnp.zeros((LANES, tile), F32)
        return carry

    lax.fori_loop(0, n_tiles, init_tile, 0)

    def pair(u):
        if isinstance(u, int):
            return pairs[u]
        return qt_ref[u], kt_ref[u]

    def scores(u, masked):
        qa, ja = pair(u)
        start = ja * tile if isinstance(ja, int) else pl.multiple_of(ja * tile, tile)
        k = k_ref[pl.ds(start, tile), :]
        if masked:
            key = lax.broadcasted_iota(jnp.int32, (tile, tile), 0)
            qry = lax.broadcasted_iota(jnp.int32, (tile, tile), 1)
            visible = key <= qry
        for c in range(2):
            s = lax.dot_general(k, qm_ref[c, qa], (((1,), (1,)), ((), ())), preferred_element_type=F32)
            if fox:
                s = s + (cq_ref[qa, c:c + 1, :] - ck_ref[pl.ds(start, tile), c:c + 1])
            if masked:
                s = jnp.where(visible, s, -jnp.inf)
            s_ref[c] = s
            mc_ref[c] = jnp.broadcast_to(jnp.max(s, axis=0, keepdims=True), (SUBLANES, tile))

    def probs(u):
        qb, _ = pair(u)
        for c in range(2):
            m_old = m_ref[c, qb]
            m_new = jnp.maximum(m_old, mc_ref[c])
            alpha = jnp.exp2(m_old - m_new)
            p = jnp.exp2(s_ref[c] - pltpu.repeat(m_new, tile // SUBLANES, 0))
            l_ref[c, qb] = alpha * l_ref[c, qb] + jnp.sum(p.reshape(tile // SUBLANES, SUBLANES, tile), axis=0)
            p_ref[c] = p.astype(BF16)
            alpha_ref[c] = alpha
            m_ref[c, qb] = m_new

    def accumulate(u):
        qc, jc = pair(u)
        vt = vt_ref[jc]
        for c in range(2):
            acc_ref[c, qc] = (pltpu.repeat(alpha_ref[c], LANES // SUBLANES, 0) * acc_ref[c, qc]
                              + jnp.dot(vt, p_ref[c], preferred_element_type=F32))

    def step(u, masked):
        static = isinstance(u, int)
        if not static or 0 <= u - 2 < n_pairs:
            accumulate(u - 2)
        if not static or 0 <= u - 1 < n_pairs:
            probs(u - 1)
        if not static or 0 <= u < n_pairs:
            scores(u, masked)

    def steady(lo, hi, masked):
        if hi - lo >= 1:
            def body(u, carry):
                step(u, masked)
                return carry
            lax.fori_loop(lo, hi, body, 0)

    for u in range(2):
        step(u, masked=u >= n_lower)
    steady(2, n_lower, masked=False)
    steady(max(2, n_lower), n_pairs, masked=True)
    for u in range(max(2, n_pairs), n_pairs + 2):
        step(u, masked=True)

    if not fox:
        lp = lam_ref[...]
        lam = (jnp.exp(jnp.sum(lp[0:1] * lp[1:2], axis=-1, keepdims=True))
               - jnp.exp(jnp.sum(lp[2:3] * lp[3:4], axis=-1, keepdims=True)) + lam_init)
        sub = sub_ref[...]

    def finish_tile(i, carry):
        o0 = acc_ref[0, i] / jnp.sum(l_ref[0, i], axis=0, keepdims=True)
        o1 = acc_ref[1, i] / jnp.sum(l_ref[1, i], axis=0, keepdims=True)
        if fox:
            feat = lax.broadcasted_iota(jnp.int32, (LANES, tile), 0)
            o = jnp.where(feat < HEAD_DIM, o0, o1)
        else:
            o = o0 - lam * o1
            o = o * lax.rsqrt(jnp.mean(o * o, axis=0, keepdims=True) + RMS_EPS) * sub * (1.0 - lam_init)
        o_ref[pl.ds(pl.multiple_of(i * tile, tile), tile), :] = o.T.astype(o_ref.dtype)
        return carry

    lax.fori_loop(0, n_tiles, finish_tile, 0)


def _attention(q, k, v, extras, *, fox, lam_init=0.0):
    b, s, _ = q.shape
    tile = min(ATTN_TILE, s)
    nt = s // tile
    lower = [(qi, ki) for qi in range(nt) for ki in range(qi)]
    pairs = tuple(lower + [(qi, qi) for qi in range(nt)])
    q_tab = jnp.asarray([pr[0] for pr in pairs], jnp.int32)
    k_tab = jnp.asarray([pr[1] for pr in pairs], jnp.int32)
    vt = jnp.transpose(v.reshape(b, nt, tile, HEAD_PAIRS, LANES), (0, 3, 1, 4, 2))
    smem = pl.BlockSpec(memory_space=pltpu.SMEM)
    seq = pl.BlockSpec((None, s, LANES), lambda bi, hp: (bi, 0, hp))
    vtspec = pl.BlockSpec((None, None, nt, LANES, tile), lambda bi, hp: (bi, hp, 0, 0, 0))
    if fox:
        extra_specs = [pl.BlockSpec((None, None, nt, 2, tile), lambda bi, hp: (bi, hp, 0, 0, 0)),
                       pl.BlockSpec((None, None, s, 2), lambda bi, hp: (bi, hp, 0, 0))]
    else:
        extra_specs = [pl.BlockSpec((4, HEAD_DIM), lambda bi, hp: (0, 0)),
                       pl.BlockSpec((LANES, 1), lambda bi, hp: (0, 0))]
    return pl.pallas_call(
        functools.partial(_attn_kernel, fox=fox, lam_init=lam_init, tile=tile, pairs=pairs, n_lower=len(lower)),
        grid=(b, HEAD_PAIRS),
        in_specs=[smem, smem, seq, seq, vtspec] + extra_specs,
        out_specs=seq,
        out_shape=jax.ShapeDtypeStruct((b, s, D_MODEL), BF16),
        scratch_shapes=[pltpu.VMEM((2, nt, tile, LANES), BF16),
                        pltpu.VMEM((2, nt, SUBLANES, tile), F32),
                        pltpu.VMEM((2, nt, SUBLANES, tile), F32),
                        pltpu.VMEM((2, nt, LANES, tile), F32),
                        pltpu.VMEM((2, tile, tile), F32),
                        pltpu.VMEM((2, SUBLANES, tile), F32),
                        pltpu.VMEM((2, tile, tile), BF16),
                        pltpu.VMEM((2, SUBLANES, tile), F32)],
        compiler_params=_params("parallel", "parallel"),
        name="fox_attn" if fox else "diff_attn",
    )(q_tab, k_tab, q, k, vt, *extras)


def _post_kernel(h_ref, o_ref, p_ref, wo_ref, gm_ref, w1_ref, w2_ref, gp_ref, wg_ref, wp_ref, gf_ref,
                 out_ref, *, final):
    h = h_ref[...] + jnp.dot(o_ref[...], wo_ref[...], preferred_element_type=F32)
    hn = _rms(h, gm_ref[...]).astype(BF16)
    for c in range(D_FF // FF_CHUNK):
        u = jnp.dot(hn, w1_ref[:, c * FF_CHUNK:(c + 1) * FF_CHUNK], preferred_element_type=F32)
        a = jnp.square(jnp.maximum(u, 0.0)).astype(BF16)
        h = h + jnp.dot(a, w2_ref[c * FF_CHUNK:(c + 1) * FF_CHUNK, :], preferred_element_type=F32)
    gate = jax.nn.sigmoid(jnp.dot(_rms(h, gp_ref[...]).astype(BF16), wg_ref[...], preferred_element_type=F32))
    emb = jnp.dot(p_ref[...].astype(BF16), wp_ref[...], preferred_element_type=F32)
    h = h + emb * gate
    if final:
        h = _rms(h, gf_ref[...])
    out_ref[...] = h


def _post(h, o, p, w_o, g_mlp, w1, w2, g_ple, w_gate, w_ple, g_final, *, final):
    n = h.shape[0]
    tm = min(ROW_TILE, n)
    row = pl.BlockSpec((tm, D_MODEL), lambda i: (i, 0))
    gain = _resident((1, D_MODEL))
    return pl.pallas_call(
        functools.partial(_post_kernel, final=final),
        grid=(n // tm,),
        in_specs=[row, row, pl.BlockSpec((tm, PLE_DIM), lambda i: (i, 0)),
                  _resident((D_MODEL, D_MODEL)), gain, _resident((D_MODEL, D_FF)), _resident((D_FF, D_MODEL)),
                  gain, _resident((D_MODEL, D_MODEL)), _resident((PLE_DIM, D_MODEL)), gain],
        out_specs=row,
        out_shape=jax.ShapeDtypeStruct((n, D_MODEL), F32),
        compiler_params=_params("parallel"),
        name="post_block",
    )(h, o, p, w_o, g_mlp.reshape(1, D_MODEL), w1, w2, g_ple.reshape(1, D_MODEL), w_gate, w_ple,
      g_final.reshape(1, D_MODEL))


def kernel(x, p, positions, a_attn_norm, a_w_qkv, a_lambda, a_subln, a_w_o, kv_norm, kv_w, kv_b_f,
           b_attn_norm, b_w_q, b_w_o, mlp_norm, mlp_w1, mlp_w2, ple_gate_norm, ple_gate_w, ple_w,
           final_norm):
    b, s, _ = x.shape
    depth = p.shape[0]
    n_a = a_w_qkv.shape[0]
    n = b * s
    bf = lambda w: w.astype(BF16)

    tabs = _rope_tables(positions)
    h = x.reshape(n, D_MODEL)
    k_sh = v_sh = c_col = c_row = None
    for i in range(depth):
        if i < n_a:
            q, k, v = _diff_proj(h, a_attn_norm[i], bf(a_w_qkv[i]), tabs)
            lam_init = 0.8 - 0.6 * math.exp(-0.3 * i)
            o = _attention(q.reshape(b, s, D_MODEL), k.reshape(b, s, D_MODEL), v.reshape(b, s, D_MODEL),
                           (a_lambda[i], a_subln[i].reshape(LANES, 1)), fox=False, lam_init=lam_init)
            w_o = a_w_o[i]
        else:
            j = i - n_a
            if j == 0:
                k_sh, v_sh, f_t = _fox_kv(h, kv_norm, bf(kv_w[:, :2 * D_MODEL]), bf(kv_w[:, 2 * D_MODEL:].T))
                c_t = _forget_cumsum(f_t, kv_b_f, b).reshape(b, HEAD_PAIRS, 2, s)
                tile = min(ATTN_TILE, s)
                c_row = jnp.swapaxes(c_t.reshape(b, HEAD_PAIRS, 2, s // tile, tile), 2, 3)
                c_col = jnp.swapaxes(c_t, 2, 3)
                k_sh = k_sh.reshape(b, s, D_MODEL)
                v_sh = v_sh.reshape(b, s, D_MODEL)
            q = _fox_q(h, b_attn_norm[j], bf(b_w_q[j]))
            o = _attention(q.reshape(b, s, D_MODEL), k_sh, v_sh, (c_row, c_col), fox=True)
            w_o = b_w_o[j]
        h = _post(h, o.reshape(n, D_MODEL), p[i].reshape(n, PLE_DIM), bf(w_o), mlp_norm[i], bf(mlp_w1[i]),
                  bf(mlp_w2[i]), ple_gate_norm[i], bf(ple_gate_w[i]), bf(ple_w[i]), final_norm,
                  final=(i == depth - 1))
    return h.reshape(b, s, D_MODEL)
```

```python
import functools
import math

import jax
import jax.numpy as jnp
from jax import lax
from jax.experimental import pallas as pl
from jax.experimental.pallas import tpu as pltpu

D_MODEL = 1024
PLE_DIM = 256
HEAD_DIM = 64
LANES = 128
SUBLANES = 8
HEAD_PAIRS = D_MODEL // LANES
FOX_HEADS = D_MODEL // HEAD_DIM
D_FF = 4 * D_MODEL
ROT_DIM = HEAD_DIM // 4
ROT_HALF = ROT_DIM // 2
ROPE_THETA = 500000.0
RMS_EPS = 1e-6
LOG2E = math.log2(math.e)
ATTN_SCALE = HEAD_DIM ** -0.5 * LOG2E

ROW_TILE = 512
ATTN_TILE = 512
FF_CHUNK = 1024
VMEM_LIMIT = 56 * 1024 * 1024

F32 = jnp.float32
BF16 = jnp.bfloat16


def _rms(x, gain):
    return x * lax.rsqrt(jnp.mean(x * x, axis=-1, keepdims=True) + RMS_EPS) * gain


def _resident(shape):
    zeros = (0,) * len(shape)
    return pl.BlockSpec(shape, lambda *_: zeros, pipeline_mode=pl.Buffered(1))


def _params(*sem):
    return pltpu.CompilerParams(dimension_semantics=sem, vmem_limit_bytes=VMEM_LIMIT)


def _rope_table_kernel(pos_ref, invf_ref, c_ref, s1_ref, s2_ref):
    ang = pos_ref[...].astype(F32) * invf_ref[...]
    cos = jnp.cos(ang)
    sin = jnp.sin(ang)
    jj = lax.broadcasted_iota(jnp.int32, ang.shape, 1) % HEAD_DIM
    c_ref[...] = jnp.where(jj < ROT_DIM, cos, 1.0)
    s1_ref[...] = jnp.where(jj < ROT_HALF, -sin, 0.0)
    s2_ref[...] = jnp.where(jj < ROT_HALF, 0.0, jnp.where(jj < ROT_DIM, sin, 0.0))


def _rope_tables(positions):
    n = positions.size
    tm = min(ROW_TILE, n)
    inv_freq = 1.0 / (ROPE_THETA ** (jnp.arange(0, ROT_DIM, 2, dtype=F32) / ROT_DIM))
    jj = jnp.arange(LANES) % HEAD_DIM
    invf = jnp.where(jj < ROT_DIM, inv_freq[jj % ROT_HALF], 0.0).reshape(1, LANES)
    row = pl.BlockSpec((tm, LANES), lambda i: (i, 0))
    return pl.pallas_call(
        _rope_table_kernel,
        grid=(n // tm,),
        in_specs=[pl.BlockSpec((tm, 1), lambda i: (i, 0)), pl.BlockSpec((1, LANES), lambda i: (0, 0))],
        out_specs=[row, row, row],
        out_shape=[jax.ShapeDtypeStruct((n, LANES), F32)] * 3,
        compiler_params=_params("parallel"),
        name="rope_tables",
    )(positions.reshape(n, 1), invf)


def _rope_store(y, c, s1, s2, scale, out_ref):
    for j in range(y.shape[1] // LANES):
        yc = y[:, j * LANES:(j + 1) * LANES]
        r = yc * c + pltpu.roll(yc, LANES - ROT_HALF, 1) * s1 + pltpu.roll(yc, ROT_HALF, 1) * s2
        out_ref[:, j * LANES:(j + 1) * LANES] = (r * scale).astype(out_ref.dtype)


def _diff_proj_kernel(h_ref, g_ref, w_ref, c_ref, s1_ref, s2_ref, q_ref, k_ref, v_ref):
    hn = _rms(h_ref[...], g_ref[...]).astype(BF16)
    c, s1, s2 = c_ref[...], s1_ref[...], s2_ref[...]
    q = jnp.dot(hn, w_ref[:, 0:D_MODEL], preferred_element_type=F32)
    _rope_store(q, c, s1, s2, ATTN_SCALE, q_ref)
    k = jnp.dot(hn, w_ref[:, D_MODEL:2 * D_MODEL], preferred_element_type=F32)
    _rope_store(k, c, s1, s2, 1.0, k_ref)
    v = jnp.dot(hn, w_ref[:, 2 * D_MODEL:3 * D_MODEL], preferred_element_type=F32)
    v_ref[...] = v.astype(v_ref.dtype)


def _diff_proj(h, gain, w_qkv, tabs):
    n = h.shape[0]
    tm = min(ROW_TILE, n)
    row = pl.BlockSpec((tm, D_MODEL), lambda i: (i, 0))
    tab = pl.BlockSpec((tm, LANES), lambda i: (i, 0))
    return pl.pallas_call(
        _diff_proj_kernel,
        grid=(n // tm,),
        in_specs=[row, _resident((1, D_MODEL)), _resident((D_MODEL, 3 * D_MODEL)), tab, tab, tab],
        out_specs=[row, row, row],
        out_shape=[jax.ShapeDtypeStruct((n, D_MODEL), BF16)] * 3,
        compiler_params=_params("parallel"),
        name="diff_proj",
    )(h, gain.reshape(1, D_MODEL), w_qkv, *tabs)


def _fox_q_kernel(h_ref, g_ref, w_ref, q_ref):
    hn = _rms(h_ref[...], g_ref[...]).astype(BF16)
    q = jnp.dot(hn, w_ref[...], preferred_element_type=F32)
    q_ref[...] = (q * ATTN_SCALE).astype(q_ref.dtype)


def _fox_q(h, gain, w_q):
    n = h.shape[0]
    tm = min(ROW_TILE, n)
    row = pl.BlockSpec((tm, D_MODEL), lambda i: (i, 0))
    return pl.pallas_call(
        _fox_q_kernel,
        grid=(n // tm,),
        in_specs=[row, _resident((1, D_MODEL)), _resident((D_MODEL, D_MODEL))],
        out_specs=row,
        out_shape=jax.ShapeDtypeStruct((n, D_MODEL), BF16),
        compiler_params=_params("parallel"),
        name="fox_q",
    )(h, gain.reshape(1, D_MODEL), w_q)


def _fox_kv_kernel(h_ref, g_ref, w_ref, wft_ref, k_ref, v_ref, ft_ref):
    hn = _rms(h_ref[...], g_ref[...]).astype(BF16)
    k_ref[...] = jnp.dot(hn, w_ref[:, 0:D_MODEL], preferred_element_type=F32).astype(k_ref.dtype)
    v_ref[...] = jnp.dot(hn, w_ref[:, D_MODEL:2 * D_MODEL], preferred_element_type=F32).astype(v_ref.dtype)
    ft_ref[...] = lax.dot_general(wft_ref[...], hn, (((1,), (1,)), ((), ())), preferred_element_type=F32)


def _fox_kv(h, gain, w_kv, w_f_t):
    n = h.shape[0]
    tm = min(ROW_TILE, n)
    row = pl.BlockSpec((tm, D_MODEL), lambda i: (i, 0))
    return pl.pallas_call(
        _fox_kv_kernel,
        grid=(n // tm,),
        in_specs=[row, _resident((1, D_MODEL)), _resident((D_MODEL, 2 * D_MODEL)), _resident((FOX_HEADS, D_MODEL))],
        out_specs=[row, row, pl.BlockSpec((FOX_HEADS, tm), lambda i: (0, i))],
        out_shape=[jax.ShapeDtypeStruct((n, D_MODEL), BF16)] * 2 + [jax.ShapeDtypeStruct((FOX_HEADS, n), F32)],
        compiler_params=_params("parallel"),
        name="fox_kv",
    )(h, gain.reshape(1, D_MODEL), w_kv, w_f_t)


def _forget_cumsum_kernel(f_ref, b_ref, c_ref):
    x = jax.nn.log_sigmoid(f_ref[...] + b_ref[...])
    s = x.shape[1]
    lane = lax.broadcasted_iota(jnp.int32, x.shape, 1)
    shift = 1
    while shift < s:
        x = x + jnp.where(lane >= shift, pltpu.roll(x, shift, 1), 0.0)
        shift *= 2
    c_ref[...] = x * LOG2E


def _forget_cumsum(f_t, bias, batch):
    seq = f_t.shape[1] // batch
    return pl.pallas_call(
        _forget_cumsum_kernel,
        grid=(batch,),
        in_specs=[pl.BlockSpec((FOX_HEADS, seq), lambda b: (0, b)), pl.BlockSpec((FOX_HEADS, 1), lambda b: (0, 0))],
        out_specs=pl.BlockSpec((None, FOX_HEADS, seq), lambda b: (b, 0, 0)),
        out_shape=jax.ShapeDtypeStruct((batch, FOX_HEADS, seq), F32),
        compiler_params=_params("parallel"),
        name="forget_cumsum",
    )(f_t, bias.reshape(FOX_HEADS, 1))


def _attn_kernel(*refs, fox, lam_init, tile, pairs, n_lower):
    if fox:
        (qt_ref, kt_ref, q_ref, k_ref, vt_ref, ck_ref, cq_ref, o_ref,
         qm_ref, m_ref, l_ref, acc_ref, s_ref, mc_ref, p_ref, alpha_ref) = refs
    else:
        (qt_ref, kt_ref, q_ref, k_ref, vt_ref, lam_ref, sub_ref, o_ref,
         qm_ref, m_ref, l_ref, acc_ref, s_ref, mc_ref, p_ref, alpha_ref) = refs
    n_tiles = q_ref.shape[0] // tile
    n_pairs = len(pairs)

    lane = lax.broadcasted_iota(jnp.int32, (tile, LANES), 1)

    def init_tile(i, carry):
        q = q_ref[pl.ds(pl.multiple_of(i * tile, tile), tile), :]
        zero = jnp.zeros_like(q)
        halves = (jnp.where(lane < HEAD_DIM, q, zero), jnp.where(lane < HEAD_DIM, zero, q))
        for c in range(2):
            qm_ref[c, i] = halves[c]
            m_ref[c, i] = jnp.full((SUBLANES, tile), -jnp.inf, F32)
            l_ref[c, i] = jnp.zeros((SUBLANES, tile), F32)
            acc_ref[c, i] = jnp.zeros((LANES, tile), F32)
        return carry

    lax.fori_loop(0, n_tiles, init_tile, 0)

    def pair(u):
        if isinstance(u, int):
            return pairs[u]
        return qt_ref[u], kt_ref[u]

    def scores(u, masked):
        qa, ja = pair(u)
        start = ja * tile if isinstance(ja, int) else pl.multiple_of(ja * tile, tile)
        k = k_ref[pl.ds(start, tile), :]
        if masked:
            key = lax.broadcasted_iota(jnp.int32, (tile, tile), 0)
            qry = lax.broadcasted_iota(jnp.int32, (tile, tile), 1)
            visible = key <= qry
        for c in range(2):
            s = lax.dot_general(k, qm_ref[c, qa], (((1,), (1,)), ((), ())), preferred_element_type=F32)
            if fox:
                s = s - ck_ref[pl.ds(start, tile), c:c + 1]
            if masked:
                s = jnp.where(visible, s, -jnp.inf)
            s_ref[c] = s
            mc_ref[c] = jnp.broadcast_to(jnp.max(s, axis=0, keepdims=True), (SUBLANES, tile))

    def probs(u):
        qb, _ = pair(u)
        for c in range(2):
            m_old = m_ref[c, qb]
            if fox:
                ct = jnp.broadcast_to(cq_ref[qb, c:c + 1, :], (SUBLANES, tile))
                m_new = jnp.maximum(m_old, mc_ref[c] + ct)
                shift = m_new - ct
            else:
                m_new = jnp.maximum(m_old, mc_ref[c])
                shift = m_new
            alpha = jnp.exp2(m_old - m_new)
            p = jnp.exp2(s_ref[c] - jnp.tile(shift, (tile // SUBLANES, 1)))
            l_ref[c, qb] = alpha * l_ref[c, qb] + jnp.sum(p.reshape(tile // SUBLANES, SUBLANES, tile), axis=0)
            p_ref[c] = p.astype(BF16)
            alpha_ref[c] = alpha
            m_ref[c, qb] = m_new

    def accumulate(u):
        qc, jc = pair(u)
        vt = vt_ref[jc]
        for c in range(2):
            acc_ref[c, qc] = (jnp.tile(alpha_ref[c], (LANES // SUBLANES, 1)) * acc_ref[c, qc]
                              + jnp.dot(vt, p_ref[c], preferred_element_type=F32))

    def step(u, masked):
        static = isinstance(u, int)
        if not static or 0 <= u - 2 < n_pairs:
            accumulate(u - 2)
        if not static or 0 <= u - 1 < n_pairs:
            probs(u - 1)
        if not static or 0 <= u < n_pairs:
            scores(u, masked)

    def steady(lo, hi, masked):
        if hi - lo >= 1:
            def body(u, carry):
                step(u, masked)
                return carry
            lax.fori_loop(lo, hi, body, 0)

    for u in range(2):
        step(u, masked=u >= n_lower)
    steady(2, n_lower, masked=False)
    steady(max(2, n_lower), n_pairs, masked=True)
    for u in range(max(2, n_pairs), n_pairs + 2):
        step(u, masked=True)

    if not fox:
        lp = lam_ref[...]
        lam = (jnp.exp(jnp.sum(lp[0:1] * lp[1:2], axis=-1, keepdims=True))
               - jnp.exp(jnp.sum(lp[2:3] * lp[3:4], axis=-1, keepdims=True)) + lam_init)
        sub = sub_ref[...]

    def finish_tile(i, carry):
        o0 = acc_ref[0, i] / jnp.sum(l_ref[0, i], axis=0, keepdims=True)
        o1 = acc_ref[1, i] / jnp.sum(l_ref[1, i], axis=0, keepdims=True)
        if fox:
            feat = lax.broadcasted_iota(jnp.int32, (LANES, tile), 0)
            o = jnp.where(feat < HEAD_DIM, o0, o1)
        else:
            o = o0 - lam * o1
            o = o * lax.rsqrt(jnp.mean(o * o, axis=0, keepdims=True) + RMS_EPS) * sub * (1.0 - lam_init)
        o_ref[pl.ds(pl.multiple_of(i * tile, tile), tile), :] = o.T.astype(o_ref.dtype)
        return carry

    lax.fori_loop(0, n_tiles, finish_tile, 0)


def _attention(q, k, v, extras, *, fox, lam_init=0.0):
    b, s, _ = q.shape
    tile = min(ATTN_TILE, s)
    nt = s // tile
    lower = [(qi, ki) for qi in range(nt) for ki in range(qi)]
    pairs = tuple(lower + [(qi, qi) for qi in range(nt)])
    q_tab = jnp.asarray([pr[0] for pr in pairs], jnp.int32)
    k_tab = jnp.asarray([pr[1] for pr in pairs], jnp.int32)
    vt = jnp.transpose(v.reshape(b, nt, tile, HEAD_PAIRS, LANES), (0, 3, 1, 4, 2))
    smem = pl.BlockSpec(memory_space=pltpu.SMEM)
    seq = pl.BlockSpec((None, s, LANES), lambda bi, hp: (bi, 0, hp))
    vtspec = pl.BlockSpec((None, None, nt, LANES, tile), lambda bi, hp: (bi, hp, 0, 0, 0))
    if fox:
        c_t, = extras
        c_q = jnp.swapaxes(c_t.reshape(b, HEAD_PAIRS, 2, nt, tile), 2, 3)
        extras = (jnp.swapaxes(c_t, 2, 3), c_q)
        extra_specs = [pl.BlockSpec((None, None, s, 2), lambda bi, hp: (bi, hp, 0, 0)),
                       pl.BlockSpec((None, None, nt, 2, tile), lambda bi, hp: (bi, hp, 0, 0, 0))]
    else:
        extra_specs = [pl.BlockSpec((4, HEAD_DIM), lambda bi, hp: (0, 0)),
                       pl.BlockSpec((LANES, 1), lambda bi, hp: (0, 0))]
    return pl.pallas_call(
        functools.partial(_attn_kernel, fox=fox, lam_init=lam_init, tile=tile, pairs=pairs, n_lower=len(lower)),
        grid=(b, HEAD_PAIRS),
        in_specs=[smem, smem, seq, seq, vtspec] + extra_specs,
        out_specs=seq,
        out_shape=jax.ShapeDtypeStruct((b, s, D_MODEL), BF16),
        scratch_shapes=[pltpu.VMEM((2, nt, tile, LANES), BF16),
                        pltpu.VMEM((2, nt, SUBLANES, tile), F32),
                        pltpu.VMEM((2, nt, SUBLANES, tile), F32),
                        pltpu.VMEM((2, nt, LANES, tile), F32),
                        pltpu.VMEM((2, tile, tile), F32),
                        pltpu.VMEM((2, SUBLANES, tile), F32),
                        pltpu.VMEM((2, tile, tile), BF16),
                        pltpu.VMEM((2, SUBLANES, tile), F32)],
        compiler_params=_params("parallel", "parallel"),
        name="fox_attn" if fox else "diff_attn",
    )(q_tab, k_tab, q, k, vt, *extras)


def _post_kernel(h_ref, o_ref, p_ref, wo_ref, gm_ref, w1_ref, w2_ref, gp_ref, wg_ref, wp_ref, gf_ref,
                 out_ref, *, final):
    h = h_ref[...] + jnp.dot(o_ref[...], wo_ref[...], preferred_element_type=F32)
    hn = _rms(h, gm_ref[...]).astype(BF16)
    for c in range(D_FF // FF_CHUNK):
        u = jnp.dot(hn, w1_ref[:, c * FF_CHUNK:(c + 1) * FF_CHUNK], preferred_element_type=F32)
        a = jnp.square(jnp.maximum(u, 0.0)).astype(BF16)
        h = h + jnp.dot(a, w2_ref[c * FF_CHUNK:(c + 1) * FF_CHUNK, :], preferred_element_type=F32)
    gate = jax.nn.sigmoid(jnp.dot(_rms(h, gp_ref[...]).astype(BF16), wg_ref[...], preferred_element_type=F32))
    emb = jnp.dot(p_ref[...].astype(BF16), wp_ref[...], preferred_element_type=F32)
    h = h + emb * gate
    if final:
        h = _rms(h, gf_ref[...])
    out_ref[...] = h


def _post(h, o, p, w_o, g_mlp, w1, w2, g_ple, w_gate, w_ple, g_final, *, final):
    n = h.shape[0]
    tm = min(ROW_TILE, n)
    row = pl.BlockSpec((tm, D_MODEL), lambda i: (i, 0))
    gain = _resident((1, D_MODEL))
    return pl.pallas_call(
        functools.partial(_post_kernel, final=final),
        grid=(n // tm,),
        in_specs=[row, row, pl.BlockSpec((tm, PLE_DIM), lambda i: (i, 0)),
                  _resident((D_MODEL, D_MODEL)), gain, _resident((D_MODEL, D_FF)), _resident((D_FF, D_MODEL)),
                  gain, _resident((D_MODEL, D_MODEL)), _resident((PLE_DIM, D_MODEL)), gain],
        out_specs=row,
        out_shape=jax.ShapeDtypeStruct((n, D_MODEL), F32),
        compiler_params=_params("parallel"),
        name="post_block",
    )(h, o, p, w_o, g_mlp.reshape(1, D_MODEL), w1, w2, g_ple.reshape(1, D_MODEL), w_gate, w_ple,
      g_final.reshape(1, D_MODEL))


def kernel(x, p, positions, a_attn_norm, a_w_qkv, a_lambda, a_subln, a_w_o, kv_norm, kv_w, kv_b_f,
           b_attn_norm, b_w_q, b_w_o, mlp_norm, mlp_w1, mlp_w2, ple_gate_norm, ple_gate_w, ple_w,
           final_norm):
    b, s, _ = x.shape
    depth = p.shape[0]
    n_a = a_w_qkv.shape[0]
    n = b * s
    bf = lambda w: w.astype(BF16)

    tabs = _rope_tables(positions)
    h = x.reshape(n, D_MODEL)
    k_sh = v_sh = c_t = None
    for i in range(depth):
        if i < n_a:
            q, k, v = _diff_proj(h, a_attn_norm[i], bf(a_w_qkv[i]), tabs)
            lam_init = 0.8 - 0.6 * math.exp(-0.3 * i)
            o = _attention(q.reshape(b, s, D_MODEL), k.reshape(b, s, D_MODEL), v.reshape(b, s, D_MODEL),
                           (a_lambda[i], a_subln[i].reshape(LANES, 1)), fox=False, lam_init=lam_init)
            w_o = a_w_o[i]
        else:
            j = i - n_a
            if j == 0:
                k_sh, v_sh, f_t = _fox_kv(h, kv_norm, bf(kv_w[:, :2 * D_MODEL]), bf(kv_w[:, 2 * D_MODEL:].T))
                c_t = _forget_cumsum(f_t, kv_b_f, b).reshape(b, HEAD_PAIRS, 2, s)
                k_sh = k_sh.reshape(b, s, D_MODEL)
                v_sh = v_sh.reshape(b, s, D_MODEL)
            q = _fox_q(h, b_attn_norm[j], bf(b_w_q[j]))
            o = _attention(q.reshape(b, s, D_MODEL), k_sh, v_sh, (c_t,), fox=True)
            w_o = b_w_o[j]
        h = _post(h, o.reshape(n, D_MODEL), p[i].reshape(n, PLE_DIM), bf(w_o), mlp_norm[i], bf(mlp_w1[i]),
                  bf(mlp_w2[i]), ple_gate_norm[i], bf(ple_gate_w[i]), bf(ple_w[i]), final_norm,
                  final=(i == depth - 1))
    return h.reshape(b, s, D_MODEL)
```

```python
import functools
import math

import jax
import jax.numpy as jnp
from jax import lax
from jax.experimental import pallas as pl
from jax.experimental.pallas import tpu as pltpu

D_MODEL = 1024
PLE_DIM = 256
HEAD_DIM = 64
LANES = 128
SUBLANES = 8
HEAD_PAIRS = D_MODEL // LANES
FOX_HEADS = D_MODEL // HEAD_DIM
D_FF = 4 * D_MODEL
ROT_DIM = HEAD_DIM // 4
ROT_HALF = ROT_DIM // 2
ROPE_THETA = 500000.0
RMS_EPS = 1e-6
LOG2E = math.log2(math.e)
ATTN_SCALE = HEAD_DIM ** -0.5 * LOG2E

ROW_TILE = 512
ATTN_TQ = 512
ATTN_TK = 512
FF_CHUNK = 1024
VMEM_LIMIT = 56 * 1024 * 1024

F32 = jnp.float32
BF16 = jnp.bfloat16


def _rms(x, gain):
    return x * lax.rsqrt(jnp.mean(x * x, axis=-1, keepdims=True) + RMS_EPS) * gain


def _resident(shape):
    zeros = (0,) * len(shape)
    return pl.BlockSpec(shape, lambda *_: zeros, pipeline_mode=pl.Buffered(1))


def _layer(stack, idx):
    tail = (0,) * (stack.ndim - 1)
    return pl.BlockSpec((None,) + stack.shape[1:], lambda *_: (idx,) + tail, pipeline_mode=pl.Buffered(1))


def _params(*sem):
    return pltpu.CompilerParams(dimension_semantics=sem, vmem_limit_bytes=VMEM_LIMIT)


def _rope_table_kernel(pos_ref, invf_ref, c_ref, s1_ref, s2_ref):
    ang = pos_ref[...].astype(F32) * invf_ref[...]
    cos = jnp.cos(ang)
    sin = jnp.sin(ang)
    jj = lax.broadcasted_iota(jnp.int32, ang.shape, 1) % HEAD_DIM
    c_ref[...] = jnp.where(jj < ROT_DIM, cos, 1.0)
    s1_ref[...] = jnp.where(jj < ROT_HALF, -sin, 0.0)
    s2_ref[...] = jnp.where(jj < ROT_HALF, 0.0, jnp.where(jj < ROT_DIM, sin, 0.0))


def _rope_tables(positions):
    n = positions.size
    tm = min(ROW_TILE, n)
    inv_freq = 1.0 / (ROPE_THETA ** (jnp.arange(0, ROT_DIM, 2, dtype=F32) / ROT_DIM))
    jj = jnp.arange(LANES) % HEAD_DIM
    invf = jnp.where(jj < ROT_DIM, inv_freq[jj % ROT_HALF], 0.0).reshape(1, LANES)
    row = pl.BlockSpec((tm, LANES), lambda i: (i, 0))
    return pl.pallas_call(
        _rope_table_kernel,
        grid=(n // tm,),
        in_specs=[pl.BlockSpec((tm, 1), lambda i: (i, 0)), pl.BlockSpec((1, LANES), lambda i: (0, 0))],
        out_specs=[row, row, row],
        out_shape=[jax.ShapeDtypeStruct((n, LANES), F32)] * 3,
        compiler_params=_params("parallel"),
        name="rope_tables",
    )(positions.reshape(n, 1), invf)


def _rope_store(y, c, s1, s2, scale, out_ref):
    for j in range(y.shape[1] // LANES):
        yc = y[:, j * LANES:(j + 1) * LANES]
        r = yc * c + pltpu.roll(yc, LANES - ROT_HALF, 1) * s1 + pltpu.roll(yc, ROT_HALF, 1) * s2
        out_ref[:, j * LANES:(j + 1) * LANES] = (r * scale).astype(out_ref.dtype)


def _store_vt(w_vt, hn, vt_ref):
    vt = lax.dot_general(w_vt, hn, (((1,), (1,)), ((), ())), preferred_element_type=F32)
    vt_ref[...] = vt.reshape(vt_ref.shape).astype(vt_ref.dtype)


def _vt_out(batch, seq, tm):
    nk = seq // tm
    spec = pl.BlockSpec((None, HEAD_PAIRS, None, LANES, tm), lambda i: (i // nk, 0, i % nk, 0, 0))
    return spec, jax.ShapeDtypeStruct((batch, HEAD_PAIRS, nk, LANES, tm), BF16)


def _diff_proj_kernel(h_ref, g_ref, w_ref, wvt_ref, c_ref, s1_ref, s2_ref, q_ref, k_ref, vt_ref):
    hn = _rms(h_ref[...], g_ref[...]).astype(BF16)
    c, s1, s2 = c_ref[...], s1_ref[...], s2_ref[...]
    q = jnp.dot(hn, w_ref[:, 0:D_MODEL], preferred_element_type=F32)
    _rope_store(q, c, s1, s2, ATTN_SCALE, q_ref)
    k = jnp.dot(hn, w_ref[:, D_MODEL:2 * D_MODEL], preferred_element_type=F32)
    _rope_store(k, c, s1, s2, 1.0, k_ref)
    _store_vt(wvt_ref[...], hn, vt_ref)


def _diff_proj(h, gains, w_qkv, w_vt, layer, tabs, batch):
    n = h.shape[0]
    tm = min(ATTN_TK, n // batch)
    row = pl.BlockSpec((tm, D_MODEL), lambda i: (i, 0))
    tab = pl.BlockSpec((tm, LANES), lambda i: (i, 0))
    vt_spec, vt_shape = _vt_out(batch, n // batch, tm)
    return pl.pallas_call(
        _diff_proj_kernel,
        grid=(n // tm,),
        in_specs=[row, _layer(gains, layer), _layer(w_qkv, layer), _layer(w_vt, layer), tab, tab, tab],
        out_specs=[row, row, vt_spec],
        out_shape=[jax.ShapeDtypeStruct((n, D_MODEL), BF16)] * 2 + [vt_shape],
        compiler_params=_params("parallel"),
        name="diff_proj",
    )(h, gains, w_qkv, w_vt, *tabs)


def _fox_q_kernel(h_ref, g_ref, w_ref, q_ref):
    hn = _rms(h_ref[...], g_ref[...]).astype(BF16)
    q = jnp.dot(hn, w_ref[...], preferred_element_type=F32)
    q_ref[...] = (q * ATTN_SCALE).astype(q_ref.dtype)


def _fox_q(h, gains, w_q, layer):
    n = h.shape[0]
    tm = min(ROW_TILE, n)
    row = pl.BlockSpec((tm, D_MODEL), lambda i: (i, 0))
    return pl.pallas_call(
        _fox_q_kernel,
        grid=(n // tm,),
        in_specs=[row, _layer(gains, layer), _layer(w_q, layer)],
        out_specs=row,
        out_shape=jax.ShapeDtypeStruct((n, D_MODEL), BF16),
        compiler_params=_params("parallel"),
        name="fox_q",
    )(h, gains, w_q)


def _fox_kv_kernel(h_ref, g_ref, wk_ref, wvt_ref, wft_ref, k_ref, vt_ref, ft_ref):
    hn = _rms(h_ref[...], g_ref[...]).astype(BF16)
    k_ref[...] = jnp.dot(hn, wk_ref[...], preferred_element_type=F32).astype(k_ref.dtype)
    _store_vt(wvt_ref[...], hn, vt_ref)
    ft_ref[...] = lax.dot_general(wft_ref[...], hn, (((1,), (1,)), ((), ())), preferred_element_type=F32)


def _fox_kv(h, gain, w_k, w_vt, w_f_t, batch):
    n = h.shape[0]
    tm = min(ATTN_TK, n // batch)
    row = pl.BlockSpec((tm, D_MODEL), lambda i: (i, 0))
    vt_spec, vt_shape = _vt_out(batch, n // batch, tm)
    return pl.pallas_call(
        _fox_kv_kernel,
        grid=(n // tm,),
        in_specs=[row, _resident((1, D_MODEL)), _resident((D_MODEL, D_MODEL)), _resident((D_MODEL, D_MODEL)),
                  _resident((FOX_HEADS, D_MODEL))],
        out_specs=[row, vt_spec, pl.BlockSpec((FOX_HEADS, tm), lambda i: (0, i))],
        out_shape=[jax.ShapeDtypeStruct((n, D_MODEL), BF16), vt_shape, jax.ShapeDtypeStruct((FOX_HEADS, n), F32)],
        compiler_params=_params("parallel"),
        name="fox_kv",
    )(h, gain.reshape(1, D_MODEL), w_k, w_vt, w_f_t)


def _forget_cumsum_kernel(f_ref, b_ref, c_ref):
    x = jax.nn.log_sigmoid(f_ref[...] + b_ref[...])
    s = x.shape[1]
    lane = lax.broadcasted_iota(jnp.int32, x.shape, 1)
    shift = 1
    while shift < s:
        x = x + jnp.where(lane >= shift, pltpu.roll(x, shift, 1), 0.0)
        shift *= 2
    c_ref[...] = x * LOG2E


def _forget_cumsum(f_t, bias, batch):
    seq = f_t.shape[1] // batch
    return pl.pallas_call(
        _forget_cumsum_kernel,
        grid=(batch,),
        in_specs=[pl.BlockSpec((FOX_HEADS, seq), lambda b: (0, b)), pl.BlockSpec((FOX_HEADS, 1), lambda b: (0, 0))],
        out_specs=pl.BlockSpec((None, FOX_HEADS, seq), lambda b: (b, 0, 0)),
        out_shape=jax.ShapeDtypeStruct((batch, FOX_HEADS, seq), F32),
        compiler_params=_params("parallel"),
        name="forget_cumsum",
    )(f_t, bias.reshape(FOX_HEADS, 1))


def _attn_kernel(*refs, fox, lam_init, tq, tk, pairs, segments):
    if fox:
        (qt_ref, kt_ref, q_ref, k_ref, vt_ref, ck_ref, cq_ref, o_ref,
         qm_ref, m_ref, l_ref, acc_ref, s_ref, mc_ref, p_ref, alpha_ref) = refs
    else:
        (qt_ref, kt_ref, q_ref, k_ref, vt_ref, lam_ref, sub_ref, o_ref,
         qm_ref, m_ref, l_ref, acc_ref, s_ref, mc_ref, p_ref, alpha_ref) = refs
    n_tiles = q_ref.shape[0] // tq
    n_pairs = len(pairs)

    lane = lax.broadcasted_iota(jnp.int32, (tq, LANES), 1)

    def init_tile(i, carry):
        q = q_ref[pl.ds(pl.multiple_of(i * tq, tq), tq), :]
        zero = jnp.zeros_like(q)
        halves = (jnp.where(lane < HEAD_DIM, q, zero), jnp.where(lane < HEAD_DIM, zero, q))
        for c in range(2):
            qm_ref[c, i] = halves[c]
            m_ref[c, i] = jnp.full((SUBLANES, tq), -jnp.inf, F32)
            l_ref[c, i] = jnp.zeros((SUBLANES, tq), F32)
            acc_ref[c, i] = jnp.zeros((LANES, tq), F32)
        return carry

    lax.fori_loop(0, n_tiles, init_tile, 0)

    def pair(u):
        if isinstance(u, int):
            return pairs[u]
        return qt_ref[u], kt_ref[u]

    def scores(u, offset):
        qa, ja = pair(u)
        start = ja * tk if isinstance(ja, int) else pl.multiple_of(ja * tk, tk)
        k = k_ref[pl.ds(start, tk), :]
        masked = offset is not None
        if masked:
            key = lax.broadcasted_iota(jnp.int32, (tk, tq), 0)
            qry = lax.broadcasted_iota(jnp.int32, (tk, tq), 1)
            visible = key + offset <= qry
        for c in range(2):
            s = lax.dot_general(k, qm_ref[c, qa], (((1,), (1,)), ((), ())), preferred_element_type=F32)
            if fox:
                s = s - ck_ref[pl.ds(start, tk), c:c + 1]
            if masked:
                s = jnp.where(visible, s, -jnp.inf)
            s_ref[c] = s
            mc_ref[c] = jnp.broadcast_to(jnp.max(s, axis=0, keepdims=True), (SUBLANES, tq))

    def probs(u):
        qb, _ = pair(u)
        for c in range(2):
            m_old = m_ref[c, qb]
            if fox:
                ct = jnp.broadcast_to(cq_ref[qb, c:c + 1, :], (SUBLANES, tq))
                m_new = jnp.maximum(m_old, mc_ref[c] + ct)
                shift = m_new - ct
            else:
                m_new = jnp.maximum(m_old, mc_ref[c])
                shift = m_new
            alpha = jnp.exp2(m_old - m_new)
            p = jnp.exp2(s_ref[c] - jnp.tile(shift, (tk // SUBLANES, 1)))
            l_ref[c, qb] = alpha * l_ref[c, qb] + jnp.sum(p.reshape(tk // SUBLANES, SUBLANES, tq), axis=0)
            p_ref[c] = p.astype(BF16)
            alpha_ref[c] = alpha
            m_ref[c, qb] = m_new

    def accumulate(u):
        qc, jc = pair(u)
        vt = vt_ref[jc]
        for c in range(2):
            acc_ref[c, qc] = (jnp.tile(alpha_ref[c], (LANES // SUBLANES, 1)) * acc_ref[c, qc]
                              + jnp.dot(vt, p_ref[c], preferred_element_type=F32))

    def step(u, offset):
        static = isinstance(u, int)
        if not static or 0 <= u - 2 < n_pairs:
            accumulate(u - 2)
        if not static or 0 <= u - 1 < n_pairs:
            probs(u - 1)
        if not static or 0 <= u < n_pairs:
            scores(u, offset)

    def offset_of(u):
        return next((off for end, off in segments if u < end), None)

    for u in range(2):
        step(u, offset_of(u))
    lo = 2
    for end, off in segments:
        if end > lo:
            def body(u, carry, off=off):
                step(u, off)
                return carry
            lax.fori_loop(lo, end, body, 0)
            lo = end
    for u in range(max(2, n_pairs), n_pairs + 2):
        step(u, None)

    if not fox:
        lp = lam_ref[...]
        lam = (jnp.exp(jnp.sum(lp[0:1] * lp[1:2], axis=-1, keepdims=True))
               - jnp.exp(jnp.sum(lp[2:3] * lp[3:4], axis=-1, keepdims=True)) + lam_init)
        sub = sub_ref[...]

    def finish_tile(i, carry):
        o0 = acc_ref[0, i] / jnp.sum(l_ref[0, i], axis=0, keepdims=True)
        o1 = acc_ref[1, i] / jnp.sum(l_ref[1, i], axis=0, keepdims=True)
        if fox:
            feat = lax.broadcasted_iota(jnp.int32, (LANES, tq), 0)
            o = jnp.where(feat < HEAD_DIM, o0, o1)
        else:
            o = o0 - lam * o1
            o = o * lax.rsqrt(jnp.mean(o * o, axis=0, keepdims=True) + RMS_EPS) * sub * (1.0 - lam_init)
        o_ref[pl.ds(pl.multiple_of(i * tq, tq), tq), :] = o.T.astype(o_ref.dtype)
        return carry

    lax.fori_loop(0, n_tiles, finish_tile, 0)


def _attention(q, k, vt, extras, *, fox, lam_init=0.0):
    b, s, _ = q.shape
    tq = min(ATTN_TQ, s)
    tk = min(ATTN_TK, tq)
    nq, nk, ratio = s // tq, s // tk, tq // tk
    pairs = [(qi, ki) for qi in range(nq) for ki in range(qi * ratio)]
    segments = [(len(pairs), None)]
    for r in range(ratio):
        pairs += [(qi, qi * ratio + r) for qi in range(nq)]
        segments.append((len(pairs), r * tk))
    pairs, segments = tuple(pairs), tuple(segments)
    q_tab = jnp.asarray([pr[0] for pr in pairs], jnp.int32)
    k_tab = jnp.asarray([pr[1] for pr in pairs], jnp.int32)
    assert vt.shape == (b, HEAD_PAIRS, nk, LANES, tk)
    smem = pl.BlockSpec(memory_space=pltpu.SMEM)
    seq = pl.BlockSpec((None, s, LANES), lambda bi, hp: (bi, 0, hp))
    vtspec = pl.BlockSpec((None, None, nk, LANES, tk), lambda bi, hp: (bi, hp, 0, 0, 0))
    if fox:
        c_t, = extras
        c_q = jnp.swapaxes(c_t.reshape(b, HEAD_PAIRS, 2, nq, tq), 2, 3)
        extras = (jnp.swapaxes(c_t, 2, 3), c_q)
        extra_specs = [pl.BlockSpec((None, None, s, 2), lambda bi, hp: (bi, hp, 0, 0)),
                       pl.BlockSpec((None, None, nq, 2, tq), lambda bi, hp: (bi, hp, 0, 0, 0))]
    else:
        extra_specs = [pl.BlockSpec((4, HEAD_DIM), lambda bi, hp: (0, 0)),
                       pl.BlockSpec((LANES, 1), lambda bi, hp: (0, 0))]
    return pl.pallas_call(
        functools.partial(_attn_kernel, fox=fox, lam_init=lam_init, tq=tq, tk=tk, pairs=pairs, segments=segments),
        grid=(b, HEAD_PAIRS),
        in_specs=[smem, smem, seq, seq, vtspec] + extra_specs,
        out_specs=seq,
        out_shape=jax.ShapeDtypeStruct((b, s, D_MODEL), BF16),
        scratch_shapes=[pltpu.VMEM((2, nq, tq, LANES), BF16),
                        pltpu.VMEM((2, nq, SUBLANES, tq), F32),
                        pltpu.VMEM((2, nq, SUBLANES, tq), F32),
                        pltpu.VMEM((2, nq, LANES, tq), F32),
                        pltpu.VMEM((2, tk, tq), F32),
                        pltpu.VMEM((2, SUBLANES, tq), F32),
                        pltpu.VMEM((2, tk, tq), BF16),
                        pltpu.VMEM((2, SUBLANES, tq), F32)],
        compiler_params=_params("parallel", "parallel"),
        name="fox_attn" if fox else "diff_attn",
    )(q_tab, k_tab, q, k, vt, *extras)


def _post_kernel(h_ref, o_ref, p_ref, wo_ref, gm_ref, w1_ref, w2_ref, gp_ref, wg_ref, wp_ref, gf_ref,
                 out_ref, *, final):
    h = h_ref[...] + jnp.dot(o_ref[...], wo_ref[...], preferred_element_type=F32)
    hn = _rms(h, gm_ref[...]).astype(BF16)
    for c in range(D_FF // FF_CHUNK):
        u = jnp.dot(hn, w1_ref[:, c * FF_CHUNK:(c + 1) * FF_CHUNK], preferred_element_type=F32)
        a = jnp.square(jnp.maximum(u, 0.0)).astype(BF16)
        h = h + jnp.dot(a, w2_ref[c * FF_CHUNK:(c + 1) * FF_CHUNK, :], preferred_element_type=F32)
    gate = jax.nn.sigmoid(jnp.dot(_rms(h, gp_ref[...]).astype(BF16), wg_ref[...], preferred_element_type=F32))
    emb = jnp.dot(p_ref[...].astype(BF16), wp_ref[...], preferred_element_type=F32)
    h = h + emb * gate
    if final:
        h = _rms(h, gf_ref[...])
    out_ref[...] = h


def _post(h, o, p, layer, w_o, o_layer, g_mlp, w1, w2, g_ple, w_gate, w_ple, g_final, *, final):
    n = h.shape[0]
    tm = min(ROW_TILE, n)
    row = pl.BlockSpec((tm, D_MODEL), lambda i: (i, 0))
    return pl.pallas_call(
        functools.partial(_post_kernel, final=final),
        grid=(n // tm,),
        in_specs=[row, row, pl.BlockSpec((None, tm, PLE_DIM), lambda i: (layer, i, 0)),
                  _layer(w_o, o_layer), _layer(g_mlp, layer), _layer(w1, layer), _layer(w2, layer),
                  _layer(g_ple, layer), _layer(w_gate, layer), _layer(w_ple, layer), _resident((1, D_MODEL))],
        out_specs=row,
        out_shape=jax.ShapeDtypeStruct((n, D_MODEL), F32),
        compiler_params=_params("parallel"),
        name="post_block",
    )(h, o, p, w_o, g_mlp, w1, w2, g_ple, w_gate, w_ple, g_final.reshape(1, D_MODEL))


def kernel(x, p, positions, a_attn_norm, a_w_qkv, a_lambda, a_subln, a_w_o, kv_norm, kv_w, kv_b_f,
           b_attn_norm, b_w_q, b_w_o, mlp_norm, mlp_w1, mlp_w2, ple_gate_norm, ple_gate_w, ple_w,
           final_norm):
    b, s, _ = x.shape
    depth = p.shape[0]
    n_a = a_w_qkv.shape[0]
    n = b * s
    bf = lambda w: w.astype(BF16)
    gains = lambda g: g.reshape(g.shape[0], 1, D_MODEL)

    a_wqkv, a_wvt = bf(a_w_qkv), bf(jnp.swapaxes(a_w_qkv[:, :, 2 * D_MODEL:], 1, 2))
    a_wo, b_wq, b_wo = bf(a_w_o), bf(b_w_q), bf(b_w_o)
    w1, w2, w_gate, w_ple = bf(mlp_w1), bf(mlp_w2), bf(ple_gate_w), bf(ple_w)
    a_gain, b_gain, g_mlp, g_ple = gains(a_attn_norm), gains(b_attn_norm), gains(mlp_norm), gains(ple_gate_norm)
    p_rows = p.reshape(depth, n, PLE_DIM)

    tabs = _rope_tables(positions)
    h = x.reshape(n, D_MODEL)
    k_sh = vt_sh = c_t = None
    for i in range(depth):
        if i < n_a:
            q, k, vt = _diff_proj(h, a_gain, a_wqkv, a_wvt, i, tabs, b)
            lam_init = 0.8 - 0.6 * math.exp(-0.3 * i)
            o = _attention(q.reshape(b, s, D_MODEL), k.reshape(b, s, D_MODEL), vt,
                           (a_lambda[i], a_subln[i].reshape(LANES, 1)), fox=False, lam_init=lam_init)
            w_o, o_layer = a_wo, i
        else:
            j = i - n_a
            if j == 0:
                k_sh, vt_sh, f_t = _fox_kv(h, kv_norm, bf(kv_w[:, :D_MODEL]), bf(kv_w[:, D_MODEL:2 * D_MODEL].T),
                                           bf(kv_w[:, 2 * D_MODEL:].T), b)
                c_t = _forget_cumsum(f_t, kv_b_f, b).reshape(b, HEAD_PAIRS, 2, s)
                k_sh = k_sh.reshape(b, s, D_MODEL)
            q = _fox_q(h, b_gain, b_wq, j)
            o = _attention(q.reshape(b, s, D_MODEL), k_sh, vt_sh, (c_t,), fox=True)
            w_o, o_layer = b_wo, j
        h = _post(h, o.reshape(n, D_MODEL), p_rows, i, w_o, o_layer, g_mlp, w1, w2, g_ple, w_gate, w_ple,
                  final_norm, final=(i == depth - 1))
    return h.reshape(b, s, D_MODEL)
```

```python
import functools
import math

import jax
import jax.numpy as jnp
from jax import lax
from jax.experimental import pallas as pl
from jax.experimental.pallas import tpu as pltpu

D_MODEL = 1024
PLE_DIM = 256
HEAD_DIM = 64
LANES = 128
SUBLANES = 8
HEAD_PAIRS = D_MODEL // LANES
FOX_HEADS = D_MODEL // HEAD_DIM
D_FF = 4 * D_MODEL
ROT_DIM = HEAD_DIM // 4
ROT_HALF = ROT_DIM // 2
ROPE_THETA = 500000.0
RMS_EPS = 1e-6
LOG2E = math.log2(math.e)
ATTN_SCALE = HEAD_DIM ** -0.5 * LOG2E

ROW_TILE = 512
ATTN_TQ = 512
ATTN_TK = 512
FF_CHUNK = 1024
VMEM_LIMIT = 56 * 1024 * 1024

F32 = jnp.float32
BF16 = jnp.bfloat16


def _rms(x, gain):
    return x * lax.rsqrt(jnp.mean(x * x, axis=-1, keepdims=True) + RMS_EPS) * gain


def _resident(shape):
    zeros = (0,) * len(shape)
    return pl.BlockSpec(shape, lambda *_: zeros, pipeline_mode=pl.Buffered(1))


def _layer(stack, idx):
    tail = (0,) * (stack.ndim - 1)
    return pl.BlockSpec((None,) + stack.shape[1:], lambda *_: (idx,) + tail, pipeline_mode=pl.Buffered(1))


def _params(*sem):
    return pltpu.CompilerParams(dimension_semantics=sem, vmem_limit_bytes=VMEM_LIMIT)


def _rope_table_kernel(pos_ref, invf_ref, c_ref, s1_ref, s2_ref):
    ang = pos_ref[...].astype(F32) * invf_ref[...]
    cos = jnp.cos(ang)
    sin = jnp.sin(ang)
    jj = lax.broadcasted_iota(jnp.int32, ang.shape, 1) % HEAD_DIM
    c_ref[...] = jnp.where(jj < ROT_DIM, cos, 1.0)
    s1_ref[...] = jnp.where(jj < ROT_HALF, -sin, 0.0)
    s2_ref[...] = jnp.where(jj < ROT_HALF, 0.0, jnp.where(jj < ROT_DIM, sin, 0.0))


def _rope_tables(positions):
    n = positions.size
    tm = min(ROW_TILE, n)
    inv_freq = 1.0 / (ROPE_THETA ** (jnp.arange(0, ROT_DIM, 2, dtype=F32) / ROT_DIM))
    jj = jnp.arange(LANES) % HEAD_DIM
    invf = jnp.where(jj < ROT_DIM, inv_freq[jj % ROT_HALF], 0.0).reshape(1, LANES)
    row = pl.BlockSpec((tm, LANES), lambda i: (i, 0))
    return pl.pallas_call(
        _rope_table_kernel,
        grid=(n // tm,),
        in_specs=[pl.BlockSpec((tm, 1), lambda i: (i, 0)), pl.BlockSpec((1, LANES), lambda i: (0, 0))],
        out_specs=[row, row, row],
        out_shape=[jax.ShapeDtypeStruct((n, LANES), F32)] * 3,
        compiler_params=_params("parallel"),
        name="rope_tables",
    )(positions.reshape(n, 1), invf)


def _rope_store(y, c, s1, s2, scale, out_ref):
    for j in range(y.shape[1] // LANES):
        yc = y[:, j * LANES:(j + 1) * LANES]
        r = yc * c + pltpu.roll(yc, LANES - ROT_HALF, 1) * s1 + pltpu.roll(yc, ROT_HALF, 1) * s2
        out_ref[:, j * LANES:(j + 1) * LANES] = (r * scale).astype(out_ref.dtype)


def _store_vt(w_vt, hn, vt_ref):
    vt = lax.dot_general(w_vt, hn, (((1,), (1,)), ((), ())), preferred_element_type=F32)
    vt_ref[...] = vt.reshape(vt_ref.shape).astype(vt_ref.dtype)


def _vt_out(batch, seq, tm):
    nk = seq // tm
    spec = pl.BlockSpec((None, HEAD_PAIRS, None, LANES, tm), lambda i: (i // nk, 0, i % nk, 0, 0))
    return spec, jax.ShapeDtypeStruct((batch, HEAD_PAIRS, nk, LANES, tm), BF16)


def _diff_proj_kernel(h_ref, g_ref, w_ref, wvt_ref, c_ref, s1_ref, s2_ref, q_ref, k_ref, vt_ref):
    hn = _rms(h_ref[...], g_ref[...]).astype(BF16)
    c, s1, s2 = c_ref[...], s1_ref[...], s2_ref[...]
    q = jnp.dot(hn, w_ref[:, 0:D_MODEL], preferred_element_type=F32)
    _rope_store(q, c, s1, s2, ATTN_SCALE, q_ref)
    k = jnp.dot(hn, w_ref[:, D_MODEL:2 * D_MODEL], preferred_element_type=F32)
    _rope_store(k, c, s1, s2, 1.0, k_ref)
    _store_vt(wvt_ref[...], hn, vt_ref)


def _diff_proj(h, gains, w_qkv, w_vt, layer, tabs, batch):
    n = h.shape[0]
    tm = min(ATTN_TK, n // batch)
    row = pl.BlockSpec((tm, D_MODEL), lambda i: (i, 0))
    tab = pl.BlockSpec((tm, LANES), lambda i: (i, 0))
    vt_spec, vt_shape = _vt_out(batch, n // batch, tm)
    return pl.pallas_call(
        _diff_proj_kernel,
        grid=(n // tm,),
        in_specs=[row, _layer(gains, layer), _layer(w_qkv, layer), _layer(w_vt, layer), tab, tab, tab],
        out_specs=[row, row, vt_spec],
        out_shape=[jax.ShapeDtypeStruct((n, D_MODEL), BF16)] * 2 + [vt_shape],
        compiler_params=_params("parallel"),
        name="diff_proj",
    )(h, gains, w_qkv, w_vt, *tabs)


def _fox_q_kernel(h_ref, g_ref, w_ref, q_ref):
    hn = _rms(h_ref[...], g_ref[...]).astype(BF16)
    q = jnp.dot(hn, w_ref[...], preferred_element_type=F32)
    q_ref[...] = (q * ATTN_SCALE).astype(q_ref.dtype)


def _fox_q(h, gains, w_q, layer):
    n = h.shape[0]
    tm = min(ROW_TILE, n)
    row = pl.BlockSpec((tm, D_MODEL), lambda i: (i, 0))
    return pl.pallas_call(
        _fox_q_kernel,
        grid=(n // tm,),
        in_specs=[row, _layer(gains, layer), _layer(w_q, layer)],
        out_specs=row,
        out_shape=jax.ShapeDtypeStruct((n, D_MODEL), BF16),
        compiler_params=_params("parallel"),
        name="fox_q",
    )(h, gains, w_q)


def _fox_kv_kernel(h_ref, g_ref, wk_ref, wvt_ref, wft_ref, k_ref, vt_ref, ft_ref):
    hn = _rms(h_ref[...], g_ref[...]).astype(BF16)
    k_ref[...] = jnp.dot(hn, wk_ref[...], preferred_element_type=F32).astype(k_ref.dtype)
    _store_vt(wvt_ref[...], hn, vt_ref)
    ft_ref[...] = lax.dot_general(wft_ref[...], hn, (((1,), (1,)), ((), ())), preferred_element_type=F32)


def _fox_kv(h, gain, w_k, w_vt, w_f_t, batch):
    n = h.shape[0]
    tm = min(ATTN_TK, n // batch)
    row = pl.BlockSpec((tm, D_MODEL), lambda i: (i, 0))
    vt_spec, vt_shape = _vt_out(batch, n // batch, tm)
    return pl.pallas_call(
        _fox_kv_kernel,
        grid=(n // tm,),
        in_specs=[row, _resident((1, D_MODEL)), _resident((D_MODEL, D_MODEL)), _resident((D_MODEL, D_MODEL)),
                  _resident((FOX_HEADS, D_MODEL))],
        out_specs=[row, vt_spec, pl.BlockSpec((FOX_HEADS, tm), lambda i: (0, i))],
        out_shape=[jax.ShapeDtypeStruct((n, D_MODEL), BF16), vt_shape, jax.ShapeDtypeStruct((FOX_HEADS, n), F32)],
        compiler_params=_params("parallel"),
        name="fox_kv",
    )(h, gain.reshape(1, D_MODEL), w_k, w_vt, w_f_t)


def _forget_cumsum_kernel(f_ref, b_ref, c_ref):
    x = jax.nn.log_sigmoid(f_ref[...] + b_ref[...])
    s = x.shape[1]
    lane = lax.broadcasted_iota(jnp.int32, x.shape, 1)
    shift = 1
    while shift < s:
        x = x + jnp.where(lane >= shift, pltpu.roll(x, shift, 1), 0.0)
        shift *= 2
    c_ref[...] = x * LOG2E


def _forget_cumsum(f_t, bias, batch):
    seq = f_t.shape[1] // batch
    return pl.pallas_call(
        _forget_cumsum_kernel,
        grid=(batch,),
        in_specs=[pl.BlockSpec((FOX_HEADS, seq), lambda b: (0, b)), pl.BlockSpec((FOX_HEADS, 1), lambda b: (0, 0))],
        out_specs=pl.BlockSpec((None, FOX_HEADS, seq), lambda b: (b, 0, 0)),
        out_shape=jax.ShapeDtypeStruct((batch, FOX_HEADS, seq), F32),
        compiler_params=_params("parallel"),
        name="forget_cumsum",
    )(f_t, bias.reshape(FOX_HEADS, 1))


def _attn_kernel(*refs, fox, lam_init, tq, tk, segments):
    if fox:
        (qt_ref, kt_ref, q_ref, k_ref, vt_ref, ck_ref, cq_ref, o_ref,
         qm_ref, m_ref, l_ref, acc_ref, s_ref, mc_ref, p_ref, alpha_ref) = refs
    else:
        (qt_ref, kt_ref, q_ref, k_ref, vt_ref, lam_ref, sub_ref, o_ref,
         qm_ref, m_ref, l_ref, acc_ref, s_ref, mc_ref, p_ref, alpha_ref) = refs
    n_tiles = q_ref.shape[0] // tq

    lane = lax.broadcasted_iota(jnp.int32, (tq, LANES), 1)

    def init_tile(i, carry):
        q = q_ref[pl.ds(pl.multiple_of(i * tq, tq), tq), :]
        zero = jnp.zeros_like(q)
        halves = (jnp.where(lane < HEAD_DIM, q, zero), jnp.where(lane < HEAD_DIM, zero, q))
        for c in range(2):
            qm_ref[c, i] = halves[c]
            m_ref[c, i] = jnp.full((SUBLANES, tq), -jnp.inf, F32)
            l_ref[c, i] = jnp.zeros((SUBLANES, tq), F32)
            acc_ref[c, i] = jnp.zeros((LANES, tq), F32)
        return carry

    lax.fori_loop(0, n_tiles, init_tile, 0)
    for c in range(2):
        qm_ref[c, n_tiles] = jnp.zeros((tq, LANES), BF16)
        m_ref[c, n_tiles] = jnp.zeros((SUBLANES, tq), F32)
        l_ref[c, n_tiles] = jnp.zeros((SUBLANES, tq), F32)
        acc_ref[c, n_tiles] = jnp.zeros((LANES, tq), F32)
    s_ref[...] = jnp.zeros(s_ref.shape, F32)
    mc_ref[...] = jnp.zeros(mc_ref.shape, F32)
    p_ref[...] = jnp.zeros(p_ref.shape, BF16)
    alpha_ref[...] = jnp.zeros(alpha_ref.shape, F32)

    def pair(u):
        return qt_ref[u], kt_ref[u]

    def scores(u, offset):
        qa, ja = pair(u)
        start = pl.multiple_of(ja * tk, tk)
        k = k_ref[pl.ds(start, tk), :]
        masked = offset is not None
        if masked:
            key = lax.broadcasted_iota(jnp.int32, (tk, tq), 0)
            qry = lax.broadcasted_iota(jnp.int32, (tk, tq), 1)
            visible = key + offset <= qry
        for c in range(2):
            s = lax.dot_general(k, qm_ref[c, qa], (((1,), (1,)), ((), ())), preferred_element_type=F32)
            if fox:
                s = s - ck_ref[pl.ds(start, tk), c:c + 1]
            if masked:
                s = jnp.where(visible, s, -jnp.inf)
            s_ref[c] = s
            mc_ref[c] = jnp.broadcast_to(jnp.max(s, axis=0, keepdims=True), (SUBLANES, tq))

    def probs(u):
        qb, _ = pair(u)
        for c in range(2):
            m_old = m_ref[c, qb]
            if fox:
                ct = jnp.broadcast_to(cq_ref[qb, c:c + 1, :], (SUBLANES, tq))
                m_new = jnp.maximum(m_old, mc_ref[c] + ct)
                shift = m_new - ct
            else:
                m_new = jnp.maximum(m_old, mc_ref[c])
                shift = m_new
            alpha = jnp.exp2(m_old - m_new)
            x = s_ref[c] - jnp.tile(shift, (tk // SUBLANES, 1))
            if fox:
                p_ref[c] = jnp.exp2(x.astype(BF16))
            else:
                p = jnp.exp2(x)
                l_ref[c, qb] = alpha * l_ref[c, qb] + jnp.sum(p.reshape(tk // SUBLANES, SUBLANES, tq), axis=0)
                p_ref[c] = p.astype(BF16)
            alpha_ref[c] = alpha
            m_ref[c, qb] = m_new

    def accumulate(u):
        qc, jc = pair(u)
        vt = vt_ref[jc]
        if fox:
            head0 = lax.broadcasted_iota(jnp.int32, vt.shape, 0) < HEAD_DIM
            ones = jnp.ones_like(vt)
        for c in range(2):
            lhs = jnp.where(head0 == (c == 0), vt, ones) if fox else vt
            acc_ref[c, qc] = (jnp.tile(alpha_ref[c], (LANES // SUBLANES, 1)) * acc_ref[c, qc]
                              + jnp.dot(lhs, p_ref[c], preferred_element_type=F32))

    lo = 2
    for end, off in segments:
        if end > lo:
            def body(u, carry, off=off):
                accumulate(u - 2)
                probs(u - 1)
                scores(u, off)
                return carry
            lax.fori_loop(lo, end, body, 0)
            lo = end

    if not fox:
        lp = lam_ref[...]
        lam = (jnp.exp(jnp.sum(lp[0:1] * lp[1:2], axis=-1, keepdims=True))
               - jnp.exp(jnp.sum(lp[2:3] * lp[3:4], axis=-1, keepdims=True)) + lam_init)
        sub = sub_ref[...]

    def finish_tile(i, carry):
        if fox:
            a0, a1 = acc_ref[0, i], acc_ref[1, i]
            feat = lax.broadcasted_iota(jnp.int32, (LANES, tq), 0)
            o = jnp.where(feat < HEAD_DIM, a0 / a0[HEAD_DIM:HEAD_DIM + 1], a1 / a1[0:1])
        else:
            o0 = acc_ref[0, i] / jnp.sum(l_ref[0, i], axis=0, keepdims=True)
            o1 = acc_ref[1, i] / jnp.sum(l_ref[1, i], axis=0, keepdims=True)
            o = o0 - lam * o1
            o = o * lax.rsqrt(jnp.mean(o * o, axis=0, keepdims=True) + RMS_EPS) * sub * (1.0 - lam_init)
        o_ref[pl.ds(pl.multiple_of(i * tq, tq), tq), :] = o.T.astype(o_ref.dtype)
        return carry

    lax.fori_loop(0, n_tiles, finish_tile, 0)


def _attention(q, k, vt, extras, *, fox, lam_init=0.0):
    b, s, _ = q.shape
    tq = min(ATTN_TQ, s)
    tk = min(ATTN_TK, tq)
    nq, nk, ratio = s // tq, s // tk, tq // tk
    dummy = [(nq, 0)] * 2
    pairs = dummy + [(qi, ki) for qi in range(nq) for ki in range(qi * ratio)]
    segments = []
    for r in range(ratio):
        pairs += dummy
        segments.append((len(pairs), None if r == 0 else (r - 1) * tk))
        pairs += [(qi, qi * ratio + r) for qi in range(nq)]
    pairs += dummy
    segments = tuple(segments + [(len(pairs), (ratio - 1) * tk)])
    q_tab = jnp.asarray([pr[0] for pr in pairs], jnp.int32)
    k_tab = jnp.asarray([pr[1] for pr in pairs], jnp.int32)
    assert vt.shape == (b, HEAD_PAIRS, nk, LANES, tk)
    smem = pl.BlockSpec(memory_space=pltpu.SMEM)
    seq = pl.BlockSpec((None, s, LANES), lambda bi, hp: (bi, 0, hp))
    vtspec = pl.BlockSpec((None, None, nk, LANES, tk), lambda bi, hp: (bi, hp, 0, 0, 0))
    if fox:
        c_t, = extras
        c_q = jnp.swapaxes(c_t.reshape(b, HEAD_PAIRS, 2, nq, tq), 2, 3)
        c_q = jnp.pad(c_q, ((0, 0), (0, 0), (0, 1), (0, 0), (0, 0)))
        extras = (jnp.swapaxes(c_t, 2, 3), c_q)
        extra_specs = [pl.BlockSpec((None, None, s, 2), lambda bi, hp: (bi, hp, 0, 0)),
                       pl.BlockSpec((None, None, nq + 1, 2, tq), lambda bi, hp: (bi, hp, 0, 0, 0))]
    else:
        extra_specs = [pl.BlockSpec((4, HEAD_DIM), lambda bi, hp: (0, 0)),
                       pl.BlockSpec((LANES, 1), lambda bi, hp: (0, 0))]
    return pl.pallas_call(
        functools.partial(_attn_kernel, fox=fox, lam_init=lam_init, tq=tq, tk=tk, segments=segments),
        grid=(b, HEAD_PAIRS),
        in_specs=[smem, smem, seq, seq, vtspec] + extra_specs,
        out_specs=seq,
        out_shape=jax.ShapeDtypeStruct((b, s, D_MODEL), BF16),
        scratch_shapes=[pltpu.VMEM((2, nq + 1, tq, LANES), BF16),
                        pltpu.VMEM((2, nq + 1, SUBLANES, tq), F32),
                        pltpu.VMEM((2, nq + 1, SUBLANES, tq), F32),
                        pltpu.VMEM((2, nq + 1, LANES, tq), F32),
                        pltpu.VMEM((2, tk, tq), F32),
                        pltpu.VMEM((2, SUBLANES, tq), F32),
                        pltpu.VMEM((2, tk, tq), BF16),
                        pltpu.VMEM((2, SUBLANES, tq), F32)],
        compiler_params=_params("parallel", "parallel"),
        name="fox_attn" if fox else "diff_attn",
    )(q_tab, k_tab, q, k, vt, *extras)


def _post_kernel(h_ref, o_ref, p_ref, wo_ref, gm_ref, w1_ref, w2_ref, gp_ref, wg_ref, wp_ref, gf_ref,
                 out_ref, *, final):
    h = h_ref[...] + jnp.dot(o_ref[...], wo_ref[...], preferred_element_type=F32)
    hn = _rms(h, gm_ref[...]).astype(BF16)
    for c in range(D_FF // FF_CHUNK):
        u = jnp.dot(hn, w1_ref[:, c * FF_CHUNK:(c + 1) * FF_CHUNK], preferred_element_type=F32)
        a = jnp.square(jnp.maximum(u, 0.0)).astype(BF16)
        h = h + jnp.dot(a, w2_ref[c * FF_CHUNK:(c + 1) * FF_CHUNK, :], preferred_element_type=F32)
    gate = jax.nn.sigmoid(jnp.dot(_rms(h, gp_ref[...]).astype(BF16), wg_ref[...], preferred_element_type=F32))
    emb = jnp.dot(p_ref[...].astype(BF16), wp_ref[...], preferred_element_type=F32)
    h = h + emb * gate
    if final:
        h = _rms(h, gf_ref[...])
    out_ref[...] = h


def _post(h, o, p, layer, w_o, o_layer, g_mlp, w1, w2, g_ple, w_gate, w_ple, g_final, *, final):
    n = h.shape[0]
    tm = min(ROW_TILE, n)
    row = pl.BlockSpec((tm, D_MODEL), lambda i: (i, 0))
    return pl.pallas_call(
        functools.partial(_post_kernel, final=final),
        grid=(n // tm,),
        in_specs=[row, row, pl.BlockSpec((None, tm, PLE_DIM), lambda i: (layer, i, 0)),
                  _layer(w_o, o_layer), _layer(g_mlp, layer), _layer(w1, layer), _layer(w2, layer),
                  _layer(g_ple, layer), _layer(w_gate, layer), _layer(w_ple, layer), _resident((1, D_MODEL))],
        out_specs=row,
        out_shape=jax.ShapeDtypeStruct((n, D_MODEL), F32),
        compiler_params=_params("parallel"),
        name="post_block",
    )(h, o, p, w_o, g_mlp, w1, w2, g_ple, w_gate, w_ple, g_final.reshape(1, D_MODEL))


def kernel(x, p, positions, a_attn_norm, a_w_qkv, a_lambda, a_subln, a_w_o, kv_norm, kv_w, kv_b_f,
           b_attn_norm, b_w_q, b_w_o, mlp_norm, mlp_w1, mlp_w2, ple_gate_norm, ple_gate_w, ple_w,
           final_norm):
    b, s, _ = x.shape
    depth = p.shape[0]
    n_a = a_w_qkv.shape[0]
    n = b * s
    bf = lambda w: w.astype(BF16)
    gains = lambda g: g.reshape(g.shape[0], 1, D_MODEL)

    a_wqkv, a_wvt = bf(a_w_qkv), bf(jnp.swapaxes(a_w_qkv[:, :, 2 * D_MODEL:], 1, 2))
    a_wo, b_wq, b_wo = bf(a_w_o), bf(b_w_q), bf(b_w_o)
    w1, w2, w_gate, w_ple = bf(mlp_w1), bf(mlp_w2), bf(ple_gate_w), bf(ple_w)
    a_gain, b_gain, g_mlp, g_ple = gains(a_attn_norm), gains(b_attn_norm), gains(mlp_norm), gains(ple_gate_norm)
    p_rows = p.reshape(depth, n, PLE_DIM)

    tabs = _rope_tables(positions)
    h = x.reshape(n, D_MODEL)
    k_sh = vt_sh = c_t = None
    for i in range(depth):
        if i < n_a:
            q, k, vt = _diff_proj(h, a_gain, a_wqkv, a_wvt, i, tabs, b)
            lam_init = 0.8 - 0.6 * math.exp(-0.3 * i)
            o = _attention(q.reshape(b, s, D_MODEL), k.reshape(b, s, D_MODEL), vt,
                           (a_lambda[i], a_subln[i].reshape(LANES, 1)), fox=False, lam_init=lam_init)
            w_o, o_layer = a_wo, i
        else:
            j = i - n_a
            if j == 0:
                k_sh, vt_sh, f_t = _fox_kv(h, kv_norm, bf(kv_w[:, :D_MODEL]), bf(kv_w[:, D_MODEL:2 * D_MODEL].T),
                                           bf(kv_w[:, 2 * D_MODEL:].T), b)
                c_t = _forget_cumsum(f_t, kv_b_f, b).reshape(b, HEAD_PAIRS, 2, s)
                k_sh = k_sh.reshape(b, s, D_MODEL)
            q = _fox_q(h, b_gain, b_wq, j)
            o = _attention(q.reshape(b, s, D_MODEL), k_sh, vt_sh, (c_t,), fox=True)
            w_o, o_layer = b_wo, j
        h = _post(h, o.reshape(n, D_MODEL), p_rows, i, w_o, o_layer, g_mlp, w1, w2, g_ple, w_gate, w_ple,
                  final_norm, final=(i == depth - 1))
    return h.reshape(b, s, D_MODEL)
```

```python
import functools
import math

import jax
import jax.numpy as jnp
from jax import lax
from jax.experimental import pallas as pl
from jax.experimental.pallas import tpu as pltpu

D_MODEL = 1024
PLE_DIM = 256
HEAD_DIM = 64
LANES = 128
SUBLANES = 8
HEAD_PAIRS = D_MODEL // LANES
FOX_HEADS = D_MODEL // HEAD_DIM
D_FF = 4 * D_MODEL
ROT_DIM = HEAD_DIM // 4
ROT_HALF = ROT_DIM // 2
ROPE_THETA = 500000.0
RMS_EPS = 1e-6
LOG2E = math.log2(math.e)
ATTN_SCALE = HEAD_DIM ** -0.5 * LOG2E

ROW_TILE = 512
ATTN_TQ = 512
ATTN_TK = 512
FF_CHUNK = 1024
VMEM_LIMIT = 56 * 1024 * 1024

F32 = jnp.float32
BF16 = jnp.bfloat16


def _rms(x, gain):
    return x * lax.rsqrt(jnp.mean(x * x, axis=-1, keepdims=True) + RMS_EPS) * gain


def _resident(shape):
    zeros = (0,) * len(shape)
    return pl.BlockSpec(shape, lambda *_: zeros, pipeline_mode=pl.Buffered(1))


def _layer(stack, idx):
    tail = (0,) * (stack.ndim - 1)
    return pl.BlockSpec((None,) + stack.shape[1:], lambda *_: (idx,) + tail, pipeline_mode=pl.Buffered(1))


def _params(*sem):
    return pltpu.CompilerParams(dimension_semantics=sem, vmem_limit_bytes=VMEM_LIMIT)


def _rope_table_kernel(pos_ref, invf_ref, c_ref, s1_ref, s2_ref):
    ang = pos_ref[...].astype(F32) * invf_ref[...]
    cos = jnp.cos(ang)
    sin = jnp.sin(ang)
    jj = lax.broadcasted_iota(jnp.int32, ang.shape, 1) % HEAD_DIM
    c_ref[...] = jnp.where(jj < ROT_DIM, cos, 1.0)
    s1_ref[...] = jnp.where(jj < ROT_HALF, -sin, 0.0)
    s2_ref[...] = jnp.where(jj < ROT_HALF, 0.0, jnp.where(jj < ROT_DIM, sin, 0.0))


def _rope_tables(positions):
    n = positions.size
    tm = min(ROW_TILE, n)
    inv_freq = 1.0 / (ROPE_THETA ** (jnp.arange(0, ROT_DIM, 2, dtype=F32) / ROT_DIM))
    jj = jnp.arange(LANES) % HEAD_DIM
    invf = jnp.where(jj < ROT_DIM, inv_freq[jj % ROT_HALF], 0.0).reshape(1, LANES)
    row = pl.BlockSpec((tm, LANES), lambda i: (i, 0))
    return pl.pallas_call(
        _rope_table_kernel,
        grid=(n // tm,),
        in_specs=[pl.BlockSpec((tm, 1), lambda i: (i, 0)), pl.BlockSpec((1, LANES), lambda i: (0, 0))],
        out_specs=[row, row, row],
        out_shape=[jax.ShapeDtypeStruct((n, LANES), F32)] * 3,
        compiler_params=_params("parallel"),
        name="rope_tables",
    )(positions.reshape(n, 1), invf)


def _rope_store(y, c, s1, s2, scale, out_ref):
    for j in range(y.shape[1] // LANES):
        yc = y[:, j * LANES:(j + 1) * LANES]
        r = yc * c + pltpu.roll(yc, LANES - ROT_HALF, 1) * s1 + pltpu.roll(yc, ROT_HALF, 1) * s2
        out_ref[:, j * LANES:(j + 1) * LANES] = (r * scale).astype(out_ref.dtype)


def _store_vt(w_vt, hn, vt_ref):
    vt = lax.dot_general(w_vt, hn, (((1,), (1,)), ((), ())), preferred_element_type=F32)
    vt_ref[...] = vt.reshape(vt_ref.shape).astype(vt_ref.dtype)


def _vt_out(batch, seq, tm):
    nk = seq // tm
    spec = pl.BlockSpec((None, HEAD_PAIRS, None, LANES, tm), lambda i: (i // nk, 0, i % nk, 0, 0))
    return spec, jax.ShapeDtypeStruct((batch, HEAD_PAIRS, nk, LANES, tm), BF16)


def _diff_proj_kernel(h_ref, g_ref, w_ref, wvt_ref, c_ref, s1_ref, s2_ref, q_ref, k_ref, vt_ref):
    hn = _rms(h_ref[...], g_ref[...]).astype(BF16)
    c, s1, s2 = c_ref[...], s1_ref[...], s2_ref[...]
    q = jnp.dot(hn, w_ref[:, 0:D_MODEL], preferred_element_type=F32)
    _rope_store(q, c, s1, s2, ATTN_SCALE, q_ref)
    k = jnp.dot(hn, w_ref[:, D_MODEL:2 * D_MODEL], preferred_element_type=F32)
    _rope_store(k, c, s1, s2, 1.0, k_ref)
    _store_vt(wvt_ref[...], hn, vt_ref)


def _diff_proj(h, gains, w_qkv, w_vt, layer, tabs, batch):
    n = h.shape[0]
    tm = min(ATTN_TK, n // batch)
    row = pl.BlockSpec((tm, D_MODEL), lambda i: (i, 0))
    tab = pl.BlockSpec((tm, LANES), lambda i: (i, 0))
    vt_spec, vt_shape = _vt_out(batch, n // batch, tm)
    return pl.pallas_call(
        _diff_proj_kernel,
        grid=(n // tm,),
        in_specs=[row, _layer(gains, layer), _layer(w_qkv, layer), _layer(w_vt, layer), tab, tab, tab],
        out_specs=[row, row, vt_spec],
        out_shape=[jax.ShapeDtypeStruct((n, D_MODEL), BF16)] * 2 + [vt_shape],
        compiler_params=_params("parallel"),
        name="diff_proj",
    )(h, gains, w_qkv, w_vt, *tabs)


def _fox_q_kernel(h_ref, g_ref, w_ref, q_ref):
    hn = _rms(h_ref[...], g_ref[...]).astype(BF16)
    q = jnp.dot(hn, w_ref[...], preferred_element_type=F32)
    q_ref[...] = (q * ATTN_SCALE).astype(q_ref.dtype)


def _fox_q(h, gains, w_q, layer):
    n = h.shape[0]
    tm = min(ROW_TILE, n)
    row = pl.BlockSpec((tm, D_MODEL), lambda i: (i, 0))
    return pl.pallas_call(
        _fox_q_kernel,
        grid=(n // tm,),
        in_specs=[row, _layer(gains, layer), _layer(w_q, layer)],
        out_specs=row,
        out_shape=jax.ShapeDtypeStruct((n, D_MODEL), BF16),
        compiler_params=_params("parallel"),
        name="fox_q",
    )(h, gains, w_q)


def _fox_kv_kernel(h_ref, g_ref, wk_ref, wvt_ref, wft_ref, k_ref, vt_ref, ft_ref):
    hn = _rms(h_ref[...], g_ref[...]).astype(BF16)
    k_ref[...] = jnp.dot(hn, wk_ref[...], preferred_element_type=F32).astype(k_ref.dtype)
    _store_vt(wvt_ref[...], hn, vt_ref)
    ft_ref[...] = lax.dot_general(wft_ref[...], hn, (((1,), (1,)), ((), ())), preferred_element_type=F32)


def _fox_kv(h, gain, w_k, w_vt, w_f_t, batch):
    n = h.shape[0]
    tm = min(ATTN_TK, n // batch)
    row = pl.BlockSpec((tm, D_MODEL), lambda i: (i, 0))
    vt_spec, vt_shape = _vt_out(batch, n // batch, tm)
    return pl.pallas_call(
        _fox_kv_kernel,
        grid=(n // tm,),
        in_specs=[row, _resident((1, D_MODEL)), _resident((D_MODEL, D_MODEL)), _resident((D_MODEL, D_MODEL)),
                  _resident((FOX_HEADS, D_MODEL))],
        out_specs=[row, vt_spec, pl.BlockSpec((FOX_HEADS, tm), lambda i: (0, i))],
        out_shape=[jax.ShapeDtypeStruct((n, D_MODEL), BF16), vt_shape, jax.ShapeDtypeStruct((FOX_HEADS, n), F32)],
        compiler_params=_params("parallel"),
        name="fox_kv",
    )(h, gain.reshape(1, D_MODEL), w_k, w_vt, w_f_t)


def _forget_cumsum_kernel(f_ref, b_ref, c_ref):
    x = jax.nn.log_sigmoid(f_ref[...] + b_ref[...])
    s = x.shape[1]
    lane = lax.broadcasted_iota(jnp.int32, x.shape, 1)
    shift = 1
    while shift < s:
        x = x + jnp.where(lane >= shift, pltpu.roll(x, shift, 1), 0.0)
        shift *= 2
    c_ref[...] = x * LOG2E


def _forget_cumsum(f_t, bias, batch):
    seq = f_t.shape[1] // batch
    return pl.pallas_call(
        _forget_cumsum_kernel,
        grid=(batch,),
        in_specs=[pl.BlockSpec((FOX_HEADS, seq), lambda b: (0, b)), pl.BlockSpec((FOX_HEADS, 1), lambda b: (0, 0))],
        out_specs=pl.BlockSpec((None, FOX_HEADS, seq), lambda b: (b, 0, 0)),
        out_shape=jax.ShapeDtypeStruct((batch, FOX_HEADS, seq), F32),
        compiler_params=_params("parallel"),
        name="forget_cumsum",
    )(f_t, bias.reshape(FOX_HEADS, 1))


def _attn_kernel(*refs, fox, lam_init, tq, tk, segments, n_streams):
    if fox:
        (qt_ref, kt_ref, q_ref, k_ref, vt_ref, ck_ref, cq_ref, o_ref,
         qm_ref, m_ref, l_ref, acc_ref, s_ref, mc_ref, p_ref, alpha_ref) = refs
    else:
        (qt_ref, kt_ref, q_ref, k_ref, vt_ref, lam_ref, sub_ref, o_ref,
         qm_ref, m_ref, l_ref, acc_ref, s_ref, mc_ref, p_ref, alpha_ref) = refs
    n_tiles = q_ref.shape[0] // tq
    n_steps = segments[-1][0]

    lane = lax.broadcasted_iota(jnp.int32, (tq, LANES), 1)

    def init_tile(i, carry):
        q = q_ref[pl.ds(pl.multiple_of(i * tq, tq), tq), :]
        zero = jnp.zeros_like(q)
        halves = (jnp.where(lane < HEAD_DIM, q, zero), jnp.where(lane < HEAD_DIM, zero, q))
        for c in range(2):
            qm_ref[c, i] = halves[c]
            m_ref[c, i] = jnp.full((SUBLANES, tq), -jnp.inf, F32)
            l_ref[c, i] = jnp.zeros((SUBLANES, tq), F32)
            acc_ref[c, i] = jnp.zeros((LANES, tq), F32)
        return carry

    lax.fori_loop(0, n_tiles, init_tile, 0)
    for c in range(2):
        for d in range(n_tiles, n_tiles + n_streams):
            qm_ref[c, d] = jnp.zeros((tq, LANES), BF16)
            m_ref[c, d] = jnp.zeros((SUBLANES, tq), F32)
            l_ref[c, d] = jnp.zeros((SUBLANES, tq), F32)
            acc_ref[c, d] = jnp.zeros((LANES, tq), F32)
    s_ref[...] = jnp.zeros(s_ref.shape, F32)
    mc_ref[...] = jnp.zeros(mc_ref.shape, F32)
    p_ref[...] = jnp.zeros(p_ref.shape, BF16)
    alpha_ref[...] = jnp.zeros(alpha_ref.shape, F32)

    def pair(st, u):
        return qt_ref[st * n_steps + u], kt_ref[st * n_steps + u]

    def scores(st, u, offset):
        qa, ja = pair(st, u)
        start = pl.multiple_of(ja * tk, tk)
        k = k_ref[pl.ds(start, tk), :]
        masked = offset is not None
        if masked:
            key = lax.broadcasted_iota(jnp.int32, (tk, tq), 0)
            qry = lax.broadcasted_iota(jnp.int32, (tk, tq), 1)
            visible = key + offset <= qry
        for c in range(2):
            s = lax.dot_general(k, qm_ref[c, qa], (((1,), (1,)), ((), ())), preferred_element_type=F32)
            if fox:
                s = s - ck_ref[pl.ds(start, tk), c:c + 1]
            if masked:
                s = jnp.where(visible, s, -jnp.inf)
            s_ref[st, c] = s
            mc_ref[st, c] = jnp.broadcast_to(jnp.max(s, axis=0, keepdims=True), (SUBLANES, tq))

    def probs(st, u):
        qb, _ = pair(st, u)
        for c in range(2):
            m_old = m_ref[c, qb]
            if fox:
                ct = jnp.broadcast_to(cq_ref[qb, c:c + 1, :], (SUBLANES, tq))
                m_new = jnp.maximum(m_old, mc_ref[st, c] + ct)
                shift = m_new - ct
            else:
                m_new = jnp.maximum(m_old, mc_ref[st, c])
                shift = m_new
            alpha = jnp.exp2(m_old - m_new)
            x = s_ref[st, c] - jnp.tile(shift, (tk // SUBLANES, 1))
            if fox:
                p_ref[st, c] = jnp.exp2(x.astype(BF16))
            else:
                p = jnp.exp2(x)
                l_ref[c, qb] = alpha * l_ref[c, qb] + jnp.sum(p.reshape(tk // SUBLANES, SUBLANES, tq), axis=0)
                p_ref[st, c] = p.astype(BF16)
            alpha_ref[st, c] = alpha
            m_ref[c, qb] = m_new

    def accumulate(st, u):
        qc, jc = pair(st, u)
        vt = vt_ref[jc]
        if fox:
            head0 = lax.broadcasted_iota(jnp.int32, vt.shape, 0) < HEAD_DIM
            ones = jnp.ones_like(vt)
        for c in range(2):
            lhs = jnp.where(head0 == (c == 0), vt, ones) if fox else vt
            acc_ref[c, qc] = (jnp.tile(alpha_ref[st, c], (LANES // SUBLANES, 1)) * acc_ref[c, qc]
                              + jnp.dot(lhs, p_ref[st, c], preferred_element_type=F32))

    lo = 2
    for end, off in segments:
        if end > lo:
            def body(u, carry, off=off):
                for st in range(n_streams):
                    accumulate(st, u - 2)
                    probs(st, u - 1)
                    scores(st, u, off)
                return carry
            lax.fori_loop(lo, end, body, 0)
            lo = end

    if not fox:
        lp = lam_ref[...]
        lam = (jnp.exp(jnp.sum(lp[0:1] * lp[1:2], axis=-1, keepdims=True))
               - jnp.exp(jnp.sum(lp[2:3] * lp[3:4], axis=-1, keepdims=True)) + lam_init)
        sub = sub_ref[...]

    def finish_tile(i, carry):
        if fox:
            a0, a1 = acc_ref[0, i], acc_ref[1, i]
            feat = lax.broadcasted_iota(jnp.int32, (LANES, tq), 0)
            o = jnp.where(feat < HEAD_DIM, a0 / a0[HEAD_DIM:HEAD_DIM + 1], a1 / a1[0:1])
        else:
            o0 = acc_ref[0, i] / jnp.sum(l_ref[0, i], axis=0, keepdims=True)
            o1 = acc_ref[1, i] / jnp.sum(l_ref[1, i], axis=0, keepdims=True)
            o = o0 - lam * o1
            o = o * lax.rsqrt(jnp.mean(o * o, axis=0, keepdims=True) + RMS_EPS) * sub * (1.0 - lam_init)
        o_ref[pl.ds(pl.multiple_of(i * tq, tq), tq), :] = o.T.astype(o_ref.dtype)
        return carry

    lax.fori_loop(0, n_tiles, finish_tile, 0)


def _attention(q, k, vt, extras, *, fox, lam_init=0.0):
    b, s, _ = q.shape
    tq = min(ATTN_TQ, s)
    tk = min(ATTN_TK, tq)
    nq, nk, ratio = s // tq, s // tk, tq // tk
    groups = ((0, 3), (1, 2)) if nq % 4 == 0 else ((0, 1, 2, 3),)
    ns = len(groups)
    q_list, k_list = [], []
    for st, residues in enumerate(groups):
        mine = [qi for qi in range(nq) if qi % 4 in residues]
        dummy = [(nq + st, 0)] * 2
        pairs = dummy + [(qi, ki) for qi in mine for ki in range(qi * ratio)]
        segments = []
        for r in range(ratio):
            pairs += dummy
            segments.append((len(pairs), None if r == 0 else (r - 1) * tk))
            pairs += [(qi, qi * ratio + r) for qi in mine]
        pairs += dummy
        segments = tuple(segments + [(len(pairs), (ratio - 1) * tk)])
        q_list += [pr[0] for pr in pairs]
        k_list += [pr[1] for pr in pairs]
    assert len(q_list) == ns * segments[-1][0]
    q_tab = jnp.asarray(q_list, jnp.int32)
    k_tab = jnp.asarray(k_list, jnp.int32)
    assert vt.shape == (b, HEAD_PAIRS, nk, LANES, tk)
    smem = pl.BlockSpec(memory_space=pltpu.SMEM)
    seq = pl.BlockSpec((None, s, LANES), lambda bi, hp: (bi, 0, hp))
    vtspec = pl.BlockSpec((None, None, nk, LANES, tk), lambda bi, hp: (bi, hp, 0, 0, 0))
    if fox:
        c_t, = extras
        c_q = jnp.swapaxes(c_t.reshape(b, HEAD_PAIRS, 2, nq, tq), 2, 3)
        c_q = jnp.pad(c_q, ((0, 0), (0, 0), (0, ns), (0, 0), (0, 0)))
        extras = (jnp.swapaxes(c_t, 2, 3), c_q)
        extra_specs = [pl.BlockSpec((None, None, s, 2), lambda bi, hp: (bi, hp, 0, 0)),
                       pl.BlockSpec((None, None, nq + ns, 2, tq), lambda bi, hp: (bi, hp, 0, 0, 0))]
    else:
        extra_specs = [pl.BlockSpec((4, HEAD_DIM), lambda bi, hp: (0, 0)),
                       pl.BlockSpec((LANES, 1), lambda bi, hp: (0, 0))]
    return pl.pallas_call(
        functools.partial(_attn_kernel, fox=fox, lam_init=lam_init, tq=tq, tk=tk, segments=segments, n_streams=ns),
        grid=(b, HEAD_PAIRS),
        in_specs=[smem, smem, seq, seq, vtspec] + extra_specs,
        out_specs=seq,
        out_shape=jax.ShapeDtypeStruct((b, s, D_MODEL), BF16),
        scratch_shapes=[pltpu.VMEM((2, nq + ns, tq, LANES), BF16),
                        pltpu.VMEM((2, nq + ns, SUBLANES, tq), F32),
                        pltpu.VMEM((2, nq + ns, SUBLANES, tq), F32),
                        pltpu.VMEM((2, nq + ns, LANES, tq), F32),
                        pltpu.VMEM((ns, 2, tk, tq), F32),
                        pltpu.VMEM((ns, 2, SUBLANES, tq), F32),
                        pltpu.VMEM((ns, 2, tk, tq), BF16),
                        pltpu.VMEM((ns, 2, SUBLANES, tq), F32)],
        compiler_params=_params("parallel", "parallel"),
        name="fox_attn" if fox else "diff_attn",
    )(q_tab, k_tab, q, k, vt, *extras)


def _post_kernel(h_ref, o_ref, p_ref, wo_ref, gm_ref, w1_ref, w2_ref, gp_ref, wg_ref, wp_ref, gf_ref,
                 out_ref, *, final):
    h = h_ref[...] + jnp.dot(o_ref[...], wo_ref[...], preferred_element_type=F32)
    hn = _rms(h, gm_ref[...]).astype(BF16)
    for c in range(D_FF // FF_CHUNK):
        u = jnp.dot(hn, w1_ref[:, c * FF_CHUNK:(c + 1) * FF_CHUNK], preferred_element_type=F32)
        a = jnp.square(jnp.maximum(u, 0.0)).astype(BF16)
        h = h + jnp.dot(a, w2_ref[c * FF_CHUNK:(c + 1) * FF_CHUNK, :], preferred_element_type=F32)
    gate = jax.nn.sigmoid(jnp.dot(_rms(h, gp_ref[...]).astype(BF16), wg_ref[...], preferred_element_type=F32))
    emb = jnp.dot(p_ref[...].astype(BF16), wp_ref[...], preferred_element_type=F32)
    h = h + emb * gate
    if final:
        h = _rms(h, gf_ref[...])
    out_ref[...] = h


def _post(h, o, p, layer, w_o, o_layer, g_mlp, w1, w2, g_ple, w_gate, w_ple, g_final, *, final):
    n = h.shape[0]
    tm = min(ROW_TILE, n)
    row = pl.BlockSpec((tm, D_MODEL), lambda i: (i, 0))
    return pl.pallas_call(
        functools.partial(_post_kernel, final=final),
        grid=(n // tm,),
        in_specs=[row, row, pl.BlockSpec((None, tm, PLE_DIM), lambda i: (layer, i, 0)),
                  _layer(w_o, o_layer), _layer(g_mlp, layer), _layer(w1, layer), _layer(w2, layer),
                  _layer(g_ple, layer), _layer(w_gate, layer), _layer(w_ple, layer), _resident((1, D_MODEL))],
        out_specs=row,
        out_shape=jax.ShapeDtypeStruct((n, D_MODEL), F32),
        compiler_params=_params("parallel"),
        name="post_block",
    )(h, o, p, w_o, g_mlp, w1, w2, g_ple, w_gate, w_ple, g_final.reshape(1, D_MODEL))


def kernel(x, p, positions, a_attn_norm, a_w_qkv, a_lambda, a_subln, a_w_o, kv_norm, kv_w, kv_b_f,
           b_attn_norm, b_w_q, b_w_o, mlp_norm, mlp_w1, mlp_w2, ple_gate_norm, ple_gate_w, ple_w,
           final_norm):
    b, s, _ = x.shape
    depth = p.shape[0]
    n_a = a_w_qkv.shape[0]
    n = b * s
    bf = lambda w: w.astype(BF16)
    gains = lambda g: g.reshape(g.shape[0], 1, D_MODEL)

    a_wqkv, a_wvt = bf(a_w_qkv), bf(jnp.swapaxes(a_w_qkv[:, :, 2 * D_MODEL:], 1, 2))
    a_wo, b_wq, b_wo = bf(a_w_o), bf(b_w_q), bf(b_w_o)
    w1, w2, w_gate, w_ple = bf(mlp_w1), bf(mlp_w2), bf(ple_gate_w), bf(ple_w)
    a_gain, b_gain, g_mlp, g_ple = gains(a_attn_norm), gains(b_attn_norm), gains(mlp_norm), gains(ple_gate_norm)
    p_rows = p.reshape(depth, n, PLE_DIM)

    tabs = _rope_tables(positions)
    h = x.reshape(n, D_MODEL)
    k_sh = vt_sh = c_t = None
    for i in range(depth):
        if i < n_a:
            q, k, vt = _diff_proj(h, a_gain, a_wqkv, a_wvt, i, tabs, b)
            lam_init = 0.8 - 0.6 * math.exp(-0.3 * i)
            o = _attention(q.reshape(b, s, D_MODEL), k.reshape(b, s, D_MODEL), vt,
                           (a_lambda[i], a_subln[i].reshape(LANES, 1)), fox=False, lam_init=lam_init)
            w_o, o_layer = a_wo, i
        else:
            j = i - n_a
            if j == 0:
                k_sh, vt_sh, f_t = _fox_kv(h, kv_norm, bf(kv_w[:, :D_MODEL]), bf(kv_w[:, D_MODEL:2 * D_MODEL].T),
                                           bf(kv_w[:, 2 * D_MODEL:].T), b)
                c_t = _forget_cumsum(f_t, kv_b_f, b).reshape(b, HEAD_PAIRS, 2, s)
                k_sh = k_sh.reshape(b, s, D_MODEL)
            q = _fox_q(h, b_gain, b_wq, j)
            o = _attention(q.reshape(b, s, D_MODEL), k_sh, vt_sh, (c_t,), fox=True)
            w_o, o_layer = b_wo, j
        h = _post(h, o.reshape(n, D_MODEL), p_rows, i, w_o, o_layer, g_mlp, w1, w2, g_ple, w_gate, w_ple,
                  final_norm, final=(i == depth - 1))
    return h.reshape(b, s, D_MODEL)
```

```python
import functools
import math

import jax
import jax.numpy as jnp
from jax import lax
from jax.experimental import pallas as pl
from jax.experimental.pallas import tpu as pltpu

D_MODEL = 1024
PLE_DIM = 256
HEAD_DIM = 64
LANES = 128
SUBLANES = 8
HEAD_PAIRS = D_MODEL // LANES
FOX_HEADS = D_MODEL // HEAD_DIM
D_FF = 4 * D_MODEL
ROT_DIM = HEAD_DIM // 4
ROT_HALF = ROT_DIM // 2
ROPE_THETA = 500000.0
RMS_EPS = 1e-6
LOG2E = math.log2(math.e)
ATTN_SCALE = HEAD_DIM ** -0.5 * LOG2E

ROW_TILE = 512
ATTN_TQ = 512
ATTN_TK = 512
FF_CHUNK = 1024
VMEM_LIMIT = 56 * 1024 * 1024

F32 = jnp.float32
BF16 = jnp.bfloat16


def _rms(x, gain):
    return x * lax.rsqrt(jnp.mean(x * x, axis=-1, keepdims=True) + RMS_EPS) * gain


def _resident(shape):
    zeros = (0,) * len(shape)
    return pl.BlockSpec(shape, lambda *_: zeros, pipeline_mode=pl.Buffered(1))


def _layer(stack, idx):
    tail = (0,) * (stack.ndim - 1)
    return pl.BlockSpec((None,) + stack.shape[1:], lambda *_: (idx,) + tail, pipeline_mode=pl.Buffered(1))


def _params(*sem):
    return pltpu.CompilerParams(dimension_semantics=sem, vmem_limit_bytes=VMEM_LIMIT)


def _rope_table_kernel(pos_ref, invf_ref, c_ref, s1_ref, s2_ref):
    ang = pos_ref[...].astype(F32) * invf_ref[...]
    cos = jnp.cos(ang)
    sin = jnp.sin(ang)
    jj = lax.broadcasted_iota(jnp.int32, ang.shape, 1) % HEAD_DIM
    c_ref[...] = jnp.where(jj < ROT_DIM, cos, 1.0)
    s1_ref[...] = jnp.where(jj < ROT_HALF, -sin, 0.0)
    s2_ref[...] = jnp.where(jj < ROT_HALF, 0.0, jnp.where(jj < ROT_DIM, sin, 0.0))


def _rope_tables(positions):
    n = positions.size
    tm = min(ROW_TILE, n)
    inv_freq = 1.0 / (ROPE_THETA ** (jnp.arange(0, ROT_DIM, 2, dtype=F32) / ROT_DIM))
    jj = jnp.arange(LANES) % HEAD_DIM
    invf = jnp.where(jj < ROT_DIM, inv_freq[jj % ROT_HALF], 0.0).reshape(1, LANES)
    row = pl.BlockSpec((tm, LANES), lambda i: (i, 0))
    return pl.pallas_call(
        _rope_table_kernel,
        grid=(n // tm,),
        in_specs=[pl.BlockSpec((tm, 1), lambda i: (i, 0)), pl.BlockSpec((1, LANES), lambda i: (0, 0))],
        out_specs=[row, row, row],
        out_shape=[jax.ShapeDtypeStruct((n, LANES), F32)] * 3,
        compiler_params=_params("parallel"),
        name="rope_tables",
    )(positions.reshape(n, 1), invf)


def _rope_store(y, c, s1, s2, scale, out_ref):
    for j in range(y.shape[1] // LANES):
        yc = y[:, j * LANES:(j + 1) * LANES]
        r = yc * c + pltpu.roll(yc, LANES - ROT_HALF, 1) * s1 + pltpu.roll(yc, ROT_HALF, 1) * s2
        out_ref[:, j * LANES:(j + 1) * LANES] = (r * scale).astype(out_ref.dtype)


def _store_vt(w_vt, hn, vt_ref):
    vt = lax.dot_general(w_vt, hn, (((1,), (1,)), ((), ())), preferred_element_type=F32)
    vt_ref[...] = vt.reshape(vt_ref.shape).astype(vt_ref.dtype)


def _vt_out(batch, seq, tm):
    nk = seq // tm
    spec = pl.BlockSpec((None, HEAD_PAIRS, None, LANES, tm), lambda i: (i // nk, 0, i % nk, 0, 0))
    return spec, jax.ShapeDtypeStruct((batch, HEAD_PAIRS, nk, LANES, tm), BF16)


def _diff_proj_kernel(h_ref, g_ref, w_ref, wvt_ref, c_ref, s1_ref, s2_ref, q_ref, k_ref, vt_ref):
    hn = _rms(h_ref[...], g_ref[...]).astype(BF16)
    c, s1, s2 = c_ref[...], s1_ref[...], s2_ref[...]
    q = jnp.dot(hn, w_ref[:, 0:D_MODEL], preferred_element_type=F32)
    _rope_store(q, c, s1, s2, ATTN_SCALE, q_ref)
    k = jnp.dot(hn, w_ref[:, D_MODEL:2 * D_MODEL], preferred_element_type=F32)
    _rope_store(k, c, s1, s2, 1.0, k_ref)
    _store_vt(wvt_ref[...], hn, vt_ref)


def _diff_proj(h, gains, w_qkv, w_vt, layer, tabs, batch):
    n = h.shape[0]
    tm = min(ATTN_TK, n // batch)
    row = pl.BlockSpec((tm, D_MODEL), lambda i: (i, 0))
    tab = pl.BlockSpec((tm, LANES), lambda i: (i, 0))
    vt_spec, vt_shape = _vt_out(batch, n // batch, tm)
    return pl.pallas_call(
        _diff_proj_kernel,
        grid=(n // tm,),
        in_specs=[row, _layer(gains, layer), _layer(w_qkv, layer), _layer(w_vt, layer), tab, tab, tab],
        out_specs=[row, row, vt_spec],
        out_shape=[jax.ShapeDtypeStruct((n, D_MODEL), BF16)] * 2 + [vt_shape],
        compiler_params=_params("parallel"),
        name="diff_proj",
    )(h, gains, w_qkv, w_vt, *tabs)


def _fox_q_kernel(h_ref, g_ref, w_ref, q_ref):
    hn = _rms(h_ref[...], g_ref[...]).astype(BF16)
    q = jnp.dot(hn, w_ref[...], preferred_element_type=F32)
    q_ref[...] = (q * ATTN_SCALE).astype(q_ref.dtype)


def _fox_q(h, gains, w_q, layer):
    n = h.shape[0]
    tm = min(ROW_TILE, n)
    row = pl.BlockSpec((tm, D_MODEL), lambda i: (i, 0))
    return pl.pallas_call(
        _fox_q_kernel,
        grid=(n // tm,),
        in_specs=[row, _layer(gains, layer), _layer(w_q, layer)],
        out_specs=row,
        out_shape=jax.ShapeDtypeStruct((n, D_MODEL), BF16),
        compiler_params=_params("parallel"),
        name="fox_q",
    )(h, gains, w_q)


def _fox_kv_kernel(h_ref, g_ref, wk_ref, wvt_ref, wft_ref, k_ref, vt_ref, ft_ref):
    hn = _rms(h_ref[...], g_ref[...]).astype(BF16)
    k_ref[...] = jnp.dot(hn, wk_ref[...], preferred_element_type=F32).astype(k_ref.dtype)
    _store_vt(wvt_ref[...], hn, vt_ref)
    ft_ref[...] = lax.dot_general(wft_ref[...], hn, (((1,), (1,)), ((), ())), preferred_element_type=F32)


def _fox_kv(h, gain, w_k, w_vt, w_f_t, batch):
    n = h.shape[0]
    tm = min(ATTN_TK, n // batch)
    row = pl.BlockSpec((tm, D_MODEL), lambda i: (i, 0))
    vt_spec, vt_shape = _vt_out(batch, n // batch, tm)
    return pl.pallas_call(
        _fox_kv_kernel,
        grid=(n // tm,),
        in_specs=[row, _resident((1, D_MODEL)), _resident((D_MODEL, D_MODEL)), _resident((D_MODEL, D_MODEL)),
                  _resident((FOX_HEADS, D_MODEL))],
        out_specs=[row, vt_spec, pl.BlockSpec((FOX_HEADS, tm), lambda i: (0, i))],
        out_shape=[jax.ShapeDtypeStruct((n, D_MODEL), BF16), vt_shape, jax.ShapeDtypeStruct((FOX_HEADS, n), F32)],
        compiler_params=_params("parallel"),
        name="fox_kv",
    )(h, gain.reshape(1, D_MODEL), w_k, w_vt, w_f_t)


def _forget_cumsum_kernel(f_ref, b_ref, c_ref):
    x = jax.nn.log_sigmoid(f_ref[...] + b_ref[...])
    s = x.shape[1]
    lane = lax.broadcasted_iota(jnp.int32, x.shape, 1)
    shift = 1
    while shift < s:
        x = x + jnp.where(lane >= shift, pltpu.roll(x, shift, 1), 0.0)
        shift *= 2
    c_ref[...] = x * LOG2E


def _forget_cumsum(f_t, bias, batch):
    seq = f_t.shape[1] // batch
    return pl.pallas_call(
        _forget_cumsum_kernel,
        grid=(batch,),
        in_specs=[pl.BlockSpec((FOX_HEADS, seq), lambda b: (0, b)), pl.BlockSpec((FOX_HEADS, 1), lambda b: (0, 0))],
        out_specs=pl.BlockSpec((None, FOX_HEADS, seq), lambda b: (b, 0, 0)),
        out_shape=jax.ShapeDtypeStruct((batch, FOX_HEADS, seq), F32),
        compiler_params=_params("parallel"),
        name="forget_cumsum",
    )(f_t, bias.reshape(FOX_HEADS, 1))


def _attn_kernel(*refs, fox, lam_init, tq, tk, segments, n_streams, n_steps):
    if fox:
        (qt_ref, kt_ref, q_ref, k_ref, vt_ref, ck_ref, cq_ref, o_ref,
         qm_ref, m_ref, l_ref, acc_ref, s_ref, mc_ref, p_ref, alpha_ref) = refs
    else:
        (qt_ref, kt_ref, q_ref, k_ref, vt_ref, lam_ref, sub_ref, o_ref,
         qm_ref, m_ref, l_ref, acc_ref, s_ref, mc_ref, p_ref, alpha_ref) = refs
    n_tiles = q_ref.shape[0] // tq

    lane = lax.broadcasted_iota(jnp.int32, (tq, LANES), 1)

    def init_tile(i, carry):
        q = q_ref[pl.ds(pl.multiple_of(i * tq, tq), tq), :]
        zero = jnp.zeros_like(q)
        halves = (jnp.where(lane < HEAD_DIM, q, zero), jnp.where(lane < HEAD_DIM, zero, q))
        for c in range(2):
            qm_ref[c, i] = halves[c]
            m_ref[c, i] = jnp.full((SUBLANES, tq), -jnp.inf, F32)
            l_ref[c, i] = jnp.zeros((SUBLANES, tq), F32)
            acc_ref[c, i] = jnp.zeros((LANES, tq), F32)
        return carry

    lax.fori_loop(0, n_tiles, init_tile, 0)
    for c in range(2):
        for d in range(n_tiles, n_tiles + n_streams):
            qm_ref[c, d] = jnp.zeros((tq, LANES), BF16)
            m_ref[c, d] = jnp.zeros((SUBLANES, tq), F32)
            l_ref[c, d] = jnp.zeros((SUBLANES, tq), F32)
            acc_ref[c, d] = jnp.zeros((LANES, tq), F32)
    s_ref[...] = jnp.zeros(s_ref.shape, F32)
    mc_ref[...] = jnp.zeros(mc_ref.shape, F32)
    p_ref[...] = jnp.zeros(p_ref.shape, BF16)
    alpha_ref[...] = jnp.zeros(alpha_ref.shape, F32)

    def pair(st, u):
        return qt_ref[st * n_steps + u], kt_ref[st * n_steps + u]

    def scores(st, u, offset):
        qa, ja = pair(st, u)
        start = pl.multiple_of(ja * tk, tk)
        k = k_ref[pl.ds(start, tk), :]
        masked = offset is not None
        if masked:
            key = lax.broadcasted_iota(jnp.int32, (tk, tq), 0)
            qry = lax.broadcasted_iota(jnp.int32, (tk, tq), 1)
            visible = key + offset <= qry
        for c in range(2):
            s = lax.dot_general(k, qm_ref[c, qa], (((1,), (1,)), ((), ())), preferred_element_type=F32)
            if fox:
                s = s - ck_ref[pl.ds(start, tk), c:c + 1]
            if masked:
                s = jnp.where(visible, s, -jnp.inf)
            s_ref[st, c] = s
            mc_ref[st, c] = jnp.broadcast_to(jnp.max(s, axis=0, keepdims=True), (SUBLANES, tq))

    def probs(st, u):
        qb, _ = pair(st, u)
        for c in range(2):
            m_old = m_ref[c, qb]
            if fox:
                ct = jnp.broadcast_to(cq_ref[qb, c:c + 1, :], (SUBLANES, tq))
                m_new = jnp.maximum(m_old, mc_ref[st, c] + ct)
                shift = m_new - ct
            else:
                m_new = jnp.maximum(m_old, mc_ref[st, c])
                shift = m_new
            alpha = jnp.exp2(m_old - m_new)
            x = s_ref[st, c] - jnp.tile(shift, (tk // SUBLANES, 1))
            if fox:
                p_ref[st, c] = jnp.exp2(x.astype(BF16))
            else:
                p = jnp.exp2(x)
                l_ref[c, qb] = alpha * l_ref[c, qb] + jnp.sum(p.reshape(tk // SUBLANES, SUBLANES, tq), axis=0)
                p_ref[st, c] = p.astype(BF16)
            alpha_ref[st, c] = alpha
            m_ref[c, qb] = m_new

    def accumulate(st, u):
        qc, jc = pair(st, u)
        vt = vt_ref[jc]
        if fox:
            head0 = lax.broadcasted_iota(jnp.int32, vt.shape, 0) < HEAD_DIM
            ones = jnp.ones_like(vt)
        for c in range(2):
            lhs = jnp.where(head0 == (c == 0), vt, ones) if fox else vt
            acc_ref[c, qc] = (jnp.tile(alpha_ref[st, c], (LANES // SUBLANES, 1)) * acc_ref[c, qc]
                              + jnp.dot(lhs, p_ref[st, c], preferred_element_type=F32))

    for start, end, off in segments:
        if end > start:
            def body(u, carry, off=off):
                for st in range(n_streams):
                    accumulate(st, u - 2)
                    probs(st, u - 1)
                    scores(st, u, off)
                return carry
            lax.fori_loop(start, end, body, 0)
        for st in range(n_streams):
            accumulate(st, end - 2)
            probs(st, end - 1)
            accumulate(st, end - 1)

    if not fox:
        lp = lam_ref[...]
        lam = (jnp.exp(jnp.sum(lp[0:1] * lp[1:2], axis=-1, keepdims=True))
               - jnp.exp(jnp.sum(lp[2:3] * lp[3:4], axis=-1, keepdims=True)) + lam_init)
        sub = sub_ref[...]

    def finish_tile(i, carry):
        if fox:
            a0, a1 = acc_ref[0, i], acc_ref[1, i]
            feat = lax.broadcasted_iota(jnp.int32, (LANES, tq), 0)
            o = jnp.where(feat < HEAD_DIM, a0 / a0[HEAD_DIM:HEAD_DIM + 1], a1 / a1[0:1])
        else:
            o0 = acc_ref[0, i] / jnp.sum(l_ref[0, i], axis=0, keepdims=True)
            o1 = acc_ref[1, i] / jnp.sum(l_ref[1, i], axis=0, keepdims=True)
            o = o0 - lam * o1
            o = o * lax.rsqrt(jnp.mean(o * o, axis=0, keepdims=True) + RMS_EPS) * sub * (1.0 - lam_init)
        o_ref[pl.ds(pl.multiple_of(i * tq, tq), tq), :] = o.T.astype(o_ref.dtype)
        return carry

    lax.fori_loop(0, n_tiles, finish_tile, 0)


def _attention(q, k, vt, extras, *, fox, lam_init=0.0):
    b, s, _ = q.shape
    tq = min(ATTN_TQ, s)
    tk = min(ATTN_TK, tq)
    nq, nk, ratio = s // tq, s // tk, tq // tk
    groups = ((0, 3), (1, 2)) if nq % 4 == 0 else ((0, 1, 2, 3),)
    ns = len(groups)
    q_list, k_list = [], []
    for st, residues in enumerate(groups):
        mine = [qi for qi in range(nq) if qi % 4 in residues]
        dummy = [(nq + st, 0)] * 2
        runs = [[(qi, ki) for qi in mine for ki in range(qi * ratio)]]
        runs += [[(qi, qi * ratio + r) for qi in mine] for r in range(ratio)]
        offsets = [None] + [r * tk for r in range(ratio)]
        pairs, segments = [], []
        for run, off in zip(runs, offsets):
            pairs += dummy
            segments.append((len(pairs), len(pairs) + len(run), off))
            pairs += run
        segments = tuple(segments)
        q_list += [pr[0] for pr in pairs]
        k_list += [pr[1] for pr in pairs]
    n_steps = len(q_list) // ns
    q_tab = jnp.asarray(q_list, jnp.int32)
    k_tab = jnp.asarray(k_list, jnp.int32)
    assert vt.shape == (b, HEAD_PAIRS, nk, LANES, tk)
    smem = pl.BlockSpec(memory_space=pltpu.SMEM)
    seq = pl.BlockSpec((None, s, LANES), lambda bi, hp: (bi, 0, hp))
    vtspec = pl.BlockSpec((None, None, nk, LANES, tk), lambda bi, hp: (bi, hp, 0, 0, 0))
    if fox:
        c_t, = extras
        c_q = jnp.swapaxes(c_t.reshape(b, HEAD_PAIRS, 2, nq, tq), 2, 3)
        c_q = jnp.pad(c_q, ((0, 0), (0, 0), (0, ns), (0, 0), (0, 0)))
        extras = (jnp.swapaxes(c_t, 2, 3), c_q)
        extra_specs = [pl.BlockSpec((None, None, s, 2), lambda bi, hp: (bi, hp, 0, 0)),
                       pl.BlockSpec((None, None, nq + ns, 2, tq), lambda bi, hp: (bi, hp, 0, 0, 0))]
    else:
        extra_specs = [pl.BlockSpec((4, HEAD_DIM), lambda bi, hp: (0, 0)),
                       pl.BlockSpec((LANES, 1), lambda bi, hp: (0, 0))]
    return pl.pallas_call(
        functools.partial(_attn_kernel, fox=fox, lam_init=lam_init, tq=tq, tk=tk, segments=segments, n_streams=ns, n_steps=n_steps),
        grid=(b, HEAD_PAIRS),
        in_specs=[smem, smem, seq, seq, vtspec] + extra_specs,
        out_specs=seq,
        out_shape=jax.ShapeDtypeStruct((b, s, D_MODEL), BF16),
        scratch_shapes=[pltpu.VMEM((2, nq + ns, tq, LANES), BF16),
                        pltpu.VMEM((2, nq + ns, SUBLANES, tq), F32),
                        pltpu.VMEM((2, nq + ns, SUBLANES, tq), F32),
                        pltpu.VMEM((2, nq + ns, LANES, tq), F32),
                        pltpu.VMEM((ns, 2, tk, tq), F32),
                        pltpu.VMEM((ns, 2, SUBLANES, tq), F32),
                        pltpu.VMEM((ns, 2, tk, tq), BF16),
                        pltpu.VMEM((ns, 2, SUBLANES, tq), F32)],
        compiler_params=_params("parallel", "parallel"),
        name="fox_attn" if fox else "diff_attn",
    )(q_tab, k_tab, q, k, vt, *extras)


def _post_kernel(h_ref, o_ref, p_ref, wo_ref, gm_ref, w1_ref, w2_ref, gp_ref, wg_ref, wp_ref, gf_ref,
                 out_ref, *, final):
    h = h_ref[...] + jnp.dot(o_ref[...], wo_ref[...], preferred_element_type=F32)
    hn = _rms(h, gm_ref[...]).astype(BF16)
    for c in range(D_FF // FF_CHUNK):
        u = jnp.dot(hn, w1_ref[:, c * FF_CHUNK:(c + 1) * FF_CHUNK], preferred_element_type=F32)
        a = jnp.square(jnp.maximum(u, 0.0)).astype(BF16)
        h = h + jnp.dot(a, w2_ref[c * FF_CHUNK:(c + 1) * FF_CHUNK, :], preferred_element_type=F32)
    gate = jax.nn.sigmoid(jnp.dot(_rms(h, gp_ref[...]).astype(BF16), wg_ref[...], preferred_element_type=F32))
    emb = jnp.dot(p_ref[...].astype(BF16), wp_ref[...], preferred_element_type=F32)
    h = h + emb * gate
    if final:
        h = _rms(h, gf_ref[...])
    out_ref[...] = h


def _post(h, o, p, layer, w_o, o_layer, g_mlp, w1, w2, g_ple, w_gate, w_ple, g_final, *, final):
    n = h.shape[0]
    tm = min(ROW_TILE, n)
    row = pl.BlockSpec((tm, D_MODEL), lambda i: (i, 0))
    return pl.pallas_call(
        functools.partial(_post_kernel, final=final),
        grid=(n // tm,),
        in_specs=[row, row, pl.BlockSpec((None, tm, PLE_DIM), lambda i: (layer, i, 0)),
                  _layer(w_o, o_layer), _layer(g_mlp, layer), _layer(w1, layer), _layer(w2, layer),
                  _layer(g_ple, layer), _layer(w_gate, layer), _layer(w_ple, layer), _resident((1, D_MODEL))],
        out_specs=row,
        out_shape=jax.ShapeDtypeStruct((n, D_MODEL), F32),
        compiler_params=_params("parallel"),
        name="post_block",
    )(h, o, p, w_o, g_mlp, w1, w2, g_ple, w_gate, w_ple, g_final.reshape(1, D_MODEL))


def kernel(x, p, positions, a_attn_norm, a_w_qkv, a_lambda, a_subln, a_w_o, kv_norm, kv_w, kv_b_f,
           b_attn_norm, b_w_q, b_w_o, mlp_norm, mlp_w1, mlp_w2, ple_gate_norm, ple_gate_w, ple_w,
           final_norm):
    b, s, _ = x.shape
    depth = p.shape[0]
    n_a = a_w_qkv.shape[0]
    n = b * s
    bf = lambda w: w.astype(BF16)
    gains = lambda g: g.reshape(g.shape[0], 1, D_MODEL)

    a_wqkv, a_wvt = bf(a_w_qkv), bf(jnp.swapaxes(a_w_qkv[:, :, 2 * D_MODEL:], 1, 2))
    a_wo, b_wq, b_wo = bf(a_w_o), bf(b_w_q), bf(b_w_o)
    w1, w2, w_gate, w_ple = bf(mlp_w1), bf(mlp_w2), bf(ple_gate_w), bf(ple_w)
    a_gain, b_gain, g_mlp, g_ple = gains(a_attn_norm), gains(b_attn_norm), gains(mlp_norm), gains(ple_gate_norm)
    p_rows = p.reshape(depth, n, PLE_DIM)

    tabs = _rope_tables(positions)
    h = x.reshape(n, D_MODEL)
    k_sh = vt_sh = c_t = None
    for i in range(depth):
        if i < n_a:
            q, k, vt = _diff_proj(h, a_gain, a_wqkv, a_wvt, i, tabs, b)
            lam_init = 0.8 - 0.6 * math.exp(-0.3 * i)
            o = _attention(q.reshape(b, s, D_MODEL), k.reshape(b, s, D_MODEL), vt,
                           (a_lambda[i], a_subln[i].reshape(LANES, 1)), fox=False, lam_init=lam_init)
            w_o, o_layer = a_wo, i
        else:
            j = i - n_a
            if j == 0:
                k_sh, vt_sh, f_t = _fox_kv(h, kv_norm, bf(kv_w[:, :D_MODEL]), bf(kv_w[:, D_MODEL:2 * D_MODEL].T),
                                           bf(kv_w[:, 2 * D_MODEL:].T), b)
                c_t = _forget_cumsum(f_t, kv_b_f, b).reshape(b, HEAD_PAIRS, 2, s)
                k_sh = k_sh.reshape(b, s, D_MODEL)
            q = _fox_q(h, b_gain, b_wq, j)
            o = _attention(q.reshape(b, s, D_MODEL), k_sh, vt_sh, (c_t,), fox=True)
            w_o, o_layer = b_wo, j
        h = _post(h, o.reshape(n, D_MODEL), p_rows, i, w_o, o_layer, g_mlp, w1, w2, g_ple, w_gate, w_ple,
                  final_norm, final=(i == depth - 1))
    return h.reshape(b, s, D_MODEL)
```

```python
import functools
import math

import jax
import jax.numpy as jnp
from jax import lax
from jax.experimental import pallas as pl
from jax.experimental.pallas import tpu as pltpu

D_MODEL = 1024
PLE_DIM = 256
HEAD_DIM = 64
LANES = 128
SUBLANES = 8
HEAD_PAIRS = D_MODEL // LANES
FOX_HEADS = D_MODEL // HEAD_DIM
D_FF = 4 * D_MODEL
ROT_DIM = HEAD_DIM // 4
ROT_HALF = ROT_DIM // 2
ROPE_THETA = 500000.0
RMS_EPS = 1e-6
LOG2E = math.log2(math.e)
ATTN_SCALE = HEAD_DIM ** -0.5 * LOG2E

ROW_TILE = 512
ATTN_TQ = 512
ATTN_TK = 512
FF_CHUNK = 1024
VMEM_LIMIT = 56 * 1024 * 1024

F32 = jnp.float32
BF16 = jnp.bfloat16


def _rms(x, gain):
    return x * lax.rsqrt(jnp.mean(x * x, axis=-1, keepdims=True) + RMS_EPS) * gain


def _resident(shape):
    zeros = (0,) * len(shape)
    return pl.BlockSpec(shape, lambda *_: zeros, pipeline_mode=pl.Buffered(1))


def _layer(stack, idx):
    tail = (0,) * (stack.ndim - 1)
    return pl.BlockSpec((None,) + stack.shape[1:], lambda *_: (idx,) + tail, pipeline_mode=pl.Buffered(1))


def _params(*sem):
    return pltpu.CompilerParams(dimension_semantics=sem, vmem_limit_bytes=VMEM_LIMIT)


def _rope_table_kernel(pos_ref, invf_ref, c_ref, s1_ref, s2_ref):
    ang = pos_ref[...].astype(F32) * invf_ref[...]
    cos = jnp.cos(ang)
    sin = jnp.sin(ang)
    jj = lax.broadcasted_iota(jnp.int32, ang.shape, 1) % HEAD_DIM
    c_ref[...] = jnp.where(jj < ROT_DIM, cos, 1.0)
    s1_ref[...] = jnp.where(jj < ROT_HALF, -sin, 0.0)
    s2_ref[...] = jnp.where(jj < ROT_HALF, 0.0, jnp.where(jj < ROT_DIM, sin, 0.0))


def _rope_tables(positions):
    n = positions.size
    tm = min(ROW_TILE, n)
    inv_freq = 1.0 / (ROPE_THETA ** (jnp.arange(0, ROT_DIM, 2, dtype=F32) / ROT_DIM))
    jj = jnp.arange(LANES) % HEAD_DIM
    invf = jnp.where(jj < ROT_DIM, inv_freq[jj % ROT_HALF], 0.0).reshape(1, LANES)
    row = pl.BlockSpec((tm, LANES), lambda i: (i, 0))
    return pl.pallas_call(
        _rope_table_kernel,
        grid=(n // tm,),
        in_specs=[pl.BlockSpec((tm, 1), lambda i: (i, 0)), pl.BlockSpec((1, LANES), lambda i: (0, 0))],
        out_specs=[row, row, row],
        out_shape=[jax.ShapeDtypeStruct((n, LANES), F32)] * 3,
        compiler_params=_params("parallel"),
        name="rope_tables",
    )(positions.reshape(n, 1), invf)


def _rope_store(y, c, s1, s2, scale, out_ref):
    for j in range(y.shape[1] // LANES):
        yc = y[:, j * LANES:(j + 1) * LANES]
        r = yc * c + pltpu.roll(yc, LANES - ROT_HALF, 1) * s1 + pltpu.roll(yc, ROT_HALF, 1) * s2
        out_ref[:, j * LANES:(j + 1) * LANES] = (r * scale).astype(out_ref.dtype)


def _store_vt(w_vt, hn, vt_ref):
    vt = lax.dot_general(w_vt, hn, (((1,), (1,)), ((), ())), preferred_element_type=F32)
    vt_ref[...] = vt.reshape(vt_ref.shape).astype(vt_ref.dtype)


def _vt_out(batch, seq, tm):
    nk = seq // tm
    spec = pl.BlockSpec((None, HEAD_PAIRS, None, LANES, tm), lambda i: (i // nk, 0, i % nk, 0, 0))
    return spec, jax.ShapeDtypeStruct((batch, HEAD_PAIRS, nk, LANES, tm), BF16)


def _diff_proj_kernel(h_ref, g_ref, w_ref, wvt_ref, c_ref, s1_ref, s2_ref, q_ref, k_ref, vt_ref):
    hn = _rms(h_ref[...], g_ref[...]).astype(BF16)
    c, s1, s2 = c_ref[...], s1_ref[...], s2_ref[...]
    q = jnp.dot(hn, w_ref[:, 0:D_MODEL], preferred_element_type=F32)
    _rope_store(q, c, s1, s2, ATTN_SCALE, q_ref)
    k = jnp.dot(hn, w_ref[:, D_MODEL:2 * D_MODEL], preferred_element_type=F32)
    _rope_store(k, c, s1, s2, 1.0, k_ref)
    _store_vt(wvt_ref[...], hn, vt_ref)


def _diff_proj(h, gains, w_qkv, w_vt, layer, tabs, batch):
    n = h.shape[0]
    tm = min(ATTN_TK, n // batch)
    row = pl.BlockSpec((tm, D_MODEL), lambda i: (i, 0))
    tab = pl.BlockSpec((tm, LANES), lambda i: (i, 0))
    vt_spec, vt_shape = _vt_out(batch, n // batch, tm)
    return pl.pallas_call(
        _diff_proj_kernel,
        grid=(n // tm,),
        in_specs=[row, _layer(gains, layer), _layer(w_qkv, layer), _layer(w_vt, layer), tab, tab, tab],
        out_specs=[row, row, vt_spec],
        out_shape=[jax.ShapeDtypeStruct((n, D_MODEL), BF16)] * 2 + [vt_shape],
        compiler_params=_params("parallel"),
        name="diff_proj",
    )(h, gains, w_qkv, w_vt, *tabs)


def _fox_q_kernel(h_ref, g_ref, w_ref, q_ref):
    hn = _rms(h_ref[...], g_ref[...]).astype(BF16)
    q = jnp.dot(hn, w_ref[...], preferred_element_type=F32)
    q_ref[...] = (q * ATTN_SCALE).astype(q_ref.dtype)


def _fox_q(h, gains, w_q, layer):
    n = h.shape[0]
    tm = min(ROW_TILE, n)
    row = pl.BlockSpec((tm, D_MODEL), lambda i: (i, 0))
    return pl.pallas_call(
        _fox_q_kernel,
        grid=(n // tm,),
        in_specs=[row, _layer(gains, layer), _layer(w_q, layer)],
        out_specs=row,
        out_shape=jax.ShapeDtypeStruct((n, D_MODEL), BF16),
        compiler_params=_params("parallel"),
        name="fox_q",
    )(h, gains, w_q)


def _fox_kv_kernel(h_ref, g_ref, wk_ref, wvt_ref, wft_ref, k_ref, vt_ref, ft_ref):
    hn = _rms(h_ref[...], g_ref[...]).astype(BF16)
    k_ref[...] = jnp.dot(hn, wk_ref[...], preferred_element_type=F32).astype(k_ref.dtype)
    _store_vt(wvt_ref[...], hn, vt_ref)
    ft_ref[...] = lax.dot_general(wft_ref[...], hn, (((1,), (1,)), ((), ())), preferred_element_type=F32)


def _fox_kv(h, gain, w_k, w_vt, w_f_t, batch):
    n = h.shape[0]
    tm = min(ATTN_TK, n // batch)
    row = pl.BlockSpec((tm, D_MODEL), lambda i: (i, 0))
    vt_spec, vt_shape = _vt_out(batch, n // batch, tm)
    return pl.pallas_call(
        _fox_kv_kernel,
        grid=(n // tm,),
        in_specs=[row, _resident((1, D_MODEL)), _resident((D_MODEL, D_MODEL)), _resident((D_MODEL, D_MODEL)),
                  _resident((FOX_HEADS, D_MODEL))],
        out_specs=[row, vt_spec, pl.BlockSpec((FOX_HEADS, tm), lambda i: (0, i))],
        out_shape=[jax.ShapeDtypeStruct((n, D_MODEL), BF16), vt_shape, jax.ShapeDtypeStruct((FOX_HEADS, n), F32)],
        compiler_params=_params("parallel"),
        name="fox_kv",
    )(h, gain.reshape(1, D_MODEL), w_k, w_vt, w_f_t)


def _forget_cumsum_kernel(f_ref, b_ref, c_ref):
    x = jax.nn.log_sigmoid(f_ref[...] + b_ref[...])
    s = x.shape[1]
    lane = lax.broadcasted_iota(jnp.int32, x.shape, 1)
    shift = 1
    while shift < s:
        x = x + jnp.where(lane >= shift, pltpu.roll(x, shift, 1), 0.0)
        shift *= 2
    c_ref[...] = x * LOG2E


def _forget_cumsum(f_t, bias, batch):
    seq = f_t.shape[1] // batch
    return pl.pallas_call(
        _forget_cumsum_kernel,
        grid=(batch,),
        in_specs=[pl.BlockSpec((FOX_HEADS, seq), lambda b: (0, b)), pl.BlockSpec((FOX_HEADS, 1), lambda b: (0, 0))],
        out_specs=pl.BlockSpec((None, FOX_HEADS, seq), lambda b: (b, 0, 0)),
        out_shape=jax.ShapeDtypeStruct((batch, FOX_HEADS, seq), F32),
        compiler_params=_params("parallel"),
        name="forget_cumsum",
    )(f_t, bias.reshape(FOX_HEADS, 1))


def _attn_kernel(*refs, fox, lam_init, tq, tk, segments, n_streams, n_steps):
    if fox:
        (qt_ref, kt_ref, q_ref, k_ref, vt_ref, ck_ref, cq_ref, o_ref,
         qm_ref, m_ref, l_ref, acc_ref, s_ref, mc_ref, p_ref, alpha_ref) = refs
    else:
        (qt_ref, kt_ref, q_ref, k_ref, vt_ref, lam_ref, sub_ref, o_ref,
         qm_ref, m_ref, l_ref, acc_ref, s_ref, mc_ref, p_ref, alpha_ref) = refs
    n_tiles = q_ref.shape[0] // tq

    lane = lax.broadcasted_iota(jnp.int32, (tq, LANES), 1)

    def init_tile(i, carry):
        q = q_ref[pl.ds(pl.multiple_of(i * tq, tq), tq), :]
        zero = jnp.zeros_like(q)
        halves = (jnp.where(lane < HEAD_DIM, q, zero), jnp.where(lane < HEAD_DIM, zero, q))
        for c in range(2):
            qm_ref[c, i] = halves[c]
            m_ref[c, i] = jnp.full((SUBLANES, tq), -jnp.inf, F32)
            l_ref[c, i] = jnp.zeros((SUBLANES, tq), F32)
            acc_ref[c, i] = jnp.zeros((LANES, tq), F32)
        return carry

    lax.fori_loop(0, n_tiles, init_tile, 0)
    for c in range(2):
        for d in range(n_tiles, n_tiles + n_streams):
            qm_ref[c, d] = jnp.zeros((tq, LANES), BF16)
            m_ref[c, d] = jnp.zeros((SUBLANES, tq), F32)
            l_ref[c, d] = jnp.zeros((SUBLANES, tq), F32)
            acc_ref[c, d] = jnp.zeros((LANES, tq), F32)
    s_ref[...] = jnp.zeros(s_ref.shape, F32)
    mc_ref[...] = jnp.zeros(mc_ref.shape, F32)
    p_ref[...] = jnp.zeros(p_ref.shape, BF16)
    alpha_ref[...] = jnp.zeros(alpha_ref.shape, F32)

    def pair(st, u):
        return qt_ref[st * n_steps + u], kt_ref[st * n_steps + u]

    def scores(st, u, offset):
        qa, ja = pair(st, u)
        start = pl.multiple_of(ja * tk, tk)
        k = k_ref[pl.ds(start, tk), :]
        masked = offset is not None
        if masked:
            key = lax.broadcasted_iota(jnp.int32, (tk, tq), 0)
            qry = lax.broadcasted_iota(jnp.int32, (tk, tq), 1)
            visible = key + offset <= qry
        for c in range(2):
            s = lax.dot_general(k, qm_ref[c, qa], (((1,), (1,)), ((), ())), preferred_element_type=F32)
            if fox:
                s = s - ck_ref[pl.ds(start, tk), c:c + 1]
            if masked:
                s = jnp.where(visible, s, -jnp.inf)
            s_ref[st, c] = s
            mc_ref[st, c] = jnp.broadcast_to(jnp.max(s, axis=0, keepdims=True), (SUBLANES, tq))

    def probs(st, u):
        qb, _ = pair(st, u)
        for c in range(2):
            m_old = m_ref[c, qb]
            if fox:
                ct = jnp.broadcast_to(cq_ref[qb, c:c + 1, :], (SUBLANES, tq))
                m_new = jnp.maximum(m_old, mc_ref[st, c] + ct)
                shift = m_new - ct
            else:
                m_new = jnp.maximum(m_old, mc_ref[st, c])
                shift = m_new
            alpha = jnp.exp2(m_old - m_new)
            x = s_ref[st, c] - jnp.tile(shift, (tk // SUBLANES, 1))
            if fox:
                p_ref[st, c] = jnp.exp2(x.astype(BF16))
            else:
                p = jnp.exp2(x)
                l_ref[c, qb] = alpha * l_ref[c, qb] + jnp.sum(p.reshape(tk // SUBLANES, SUBLANES, tq), axis=0)
                p_ref[st, c] = p.astype(BF16)
            alpha_ref[st, c] = alpha
            m_ref[c, qb] = m_new

    def accumulate(st, u):
        qc, jc = pair(st, u)
        vt = vt_ref[jc]
        if fox:
            head0 = lax.broadcasted_iota(jnp.int32, vt.shape, 0) < HEAD_DIM
            ones = jnp.ones_like(vt)
        for c in range(2):
            lhs = jnp.where(head0 == (c == 0), vt, ones) if fox else vt
            acc_ref[c, qc] = (jnp.tile(alpha_ref[st, c], (LANES // SUBLANES, 1)) * acc_ref[c, qc]
                              + jnp.dot(lhs, p_ref[st, c], preferred_element_type=F32))

    for start, end, off in segments:
        if end > start:
            def body(u, carry, off=off):
                stages = (lambda st: accumulate(st, u - 2), lambda st: probs(st, u - 1),
                          lambda st: scores(st, u, off))
                if fox:
                    for stage in stages:
                        for st in range(n_streams):
                            stage(st)
                else:
                    for st in range(n_streams):
                        for stage in stages:
                            stage(st)
                return carry
            lax.fori_loop(start, end, body, 0)
        for st in range(n_streams):
            accumulate(st, end - 2)
            probs(st, end - 1)
            accumulate(st, end - 1)

    if not fox:
        lp = lam_ref[...]
        lam = (jnp.exp(jnp.sum(lp[0:1] * lp[1:2], axis=-1, keepdims=True))
               - jnp.exp(jnp.sum(lp[2:3] * lp[3:4], axis=-1, keepdims=True)) + lam_init)
        sub = sub_ref[...]

    def finish_tile(i, carry):
        if fox:
            a0, a1 = acc_ref[0, i], acc_ref[1, i]
            feat = lax.broadcasted_iota(jnp.int32, (LANES, tq), 0)
            o = jnp.where(feat < HEAD_DIM, a0 / a0[HEAD_DIM:HEAD_DIM + 1], a1 / a1[0:1])
        else:
            o0 = acc_ref[0, i] / jnp.sum(l_ref[0, i], axis=0, keepdims=True)
            o1 = acc_ref[1, i] / jnp.sum(l_ref[1, i], axis=0, keepdims=True)
            o = o0 - lam * o1
            o = o * lax.rsqrt(jnp.mean(o * o, axis=0, keepdims=True) + RMS_EPS) * sub * (1.0 - lam_init)
        o_ref[pl.ds(pl.multiple_of(i * tq, tq), tq), :] = o.T.astype(o_ref.dtype)
        return carry

    lax.fori_loop(0, n_tiles, finish_tile, 0)


def _attention(q, k, vt, extras, *, fox, lam_init=0.0):
    b, s, _ = q.shape
    tq = min(ATTN_TQ, s)
    tk = min(ATTN_TK, tq)
    nq, nk, ratio = s // tq, s // tk, tq // tk
    groups = ((0, 3), (1, 2)) if nq % 4 == 0 else ((0, 1, 2, 3),)
    ns = len(groups)
    q_list, k_list = [], []
    for st, residues in enumerate(groups):
        mine = [qi for qi in range(nq) if qi % 4 in residues]
        dummy = [(nq + st, 0)] * 2
        runs = [[(qi, ki) for qi in mine for ki in range(qi * ratio)]]
        runs += [[(qi, qi * ratio + r) for qi in mine] for r in range(ratio)]
        offsets = [None] + [r * tk for r in range(ratio)]
        pairs, segments = [], []
        for run, off in zip(runs, offsets):
            pairs += dummy
            segments.append((len(pairs), len(pairs) + len(run), off))
            pairs += run
        segments = tuple(segments)
        q_list += [pr[0] for pr in pairs]
        k_list += [pr[1] for pr in pairs]
    n_steps = len(q_list) // ns
    q_tab = jnp.asarray(q_list, jnp.int32)
    k_tab = jnp.asarray(k_list, jnp.int32)
    assert vt.shape == (b, HEAD_PAIRS, nk, LANES, tk)
    smem = pl.BlockSpec(memory_space=pltpu.SMEM)
    seq = pl.BlockSpec((None, s, LANES), lambda bi, hp: (bi, 0, hp))
    vtspec = pl.BlockSpec((None, None, nk, LANES, tk), lambda bi, hp: (bi, hp, 0, 0, 0))
    if fox:
        c_t, = extras
        c_q = jnp.swapaxes(c_t.reshape(b, HEAD_PAIRS, 2, nq, tq), 2, 3)
        c_q = jnp.pad(c_q, ((0, 0), (0, 0), (0, ns), (0, 0), (0, 0)))
        extras = (jnp.swapaxes(c_t, 2, 3), c_q)
        extra_specs = [pl.BlockSpec((None, None, s, 2), lambda bi, hp: (bi, hp, 0, 0)),
                       pl.BlockSpec((None, None, nq + ns, 2, tq), lambda bi, hp: (bi, hp, 0, 0, 0))]
    else:
        extra_specs = [pl.BlockSpec((4, HEAD_DIM), lambda bi, hp: (0, 0)),
                       pl.BlockSpec((LANES, 1), lambda bi, hp: (0, 0))]
    return pl.pallas_call(
        functools.partial(_attn_kernel, fox=fox, lam_init=lam_init, tq=tq, tk=tk, segments=segments, n_streams=ns, n_steps=n_steps),
        grid=(b, HEAD_PAIRS),
        in_specs=[smem, smem, seq, seq, vtspec] + extra_specs,
        out_specs=seq,
        out_shape=jax.ShapeDtypeStruct((b, s, D_MODEL), BF16),
        scratch_shapes=[pltpu.VMEM((2, nq + ns, tq, LANES), BF16),
                        pltpu.VMEM((2, nq + ns, SUBLANES, tq), F32),
                        pltpu.VMEM((2, nq + ns, SUBLANES, tq), F32),
                        pltpu.VMEM((2, nq + ns, LANES, tq), F32),
                        pltpu.VMEM((ns, 2, tk, tq), F32),
                        pltpu.VMEM((ns, 2, SUBLANES, tq), F32),
                        pltpu.VMEM((ns, 2, tk, tq), BF16),
                        pltpu.VMEM((ns, 2, SUBLANES, tq), F32)],
        compiler_params=_params("parallel", "parallel"),
        name="fox_attn" if fox else "diff_attn",
    )(q_tab, k_tab, q, k, vt, *extras)


def _post_kernel(h_ref, o_ref, p_ref, wo_ref, gm_ref, w1_ref, w2_ref, gp_ref, wg_ref, wp_ref, gf_ref,
                 out_ref, *, final):
    h = h_ref[...] + jnp.dot(o_ref[...], wo_ref[...], preferred_element_type=F32)
    hn = _rms(h, gm_ref[...]).astype(BF16)
    for c in range(D_FF // FF_CHUNK):
        u = jnp.dot(hn, w1_ref[:, c * FF_CHUNK:(c + 1) * FF_CHUNK], preferred_element_type=F32)
        a = jnp.square(jnp.maximum(u, 0.0)).astype(BF16)
        h = h + jnp.dot(a, w2_ref[c * FF_CHUNK:(c + 1) * FF_CHUNK, :], preferred_element_type=F32)
    gate = jax.nn.sigmoid(jnp.dot(_rms(h, gp_ref[...]).astype(BF16), wg_ref[...], preferred_element_type=F32))
    emb = jnp.dot(p_ref[...].astype(BF16), wp_ref[...], preferred_element_type=F32)
    h = h + emb * gate
    if final:
        h = _rms(h, gf_ref[...])
    out_ref[...] = h


def _post(h, o, p, layer, w_o, o_layer, g_mlp, w1, w2, g_ple, w_gate, w_ple, g_final, *, final):
    n = h.shape[0]
    tm = min(ROW_TILE, n)
    row = pl.BlockSpec((tm, D_MODEL), lambda i: (i, 0))
    return pl.pallas_call(
        functools.partial(_post_kernel, final=final),
        grid=(n // tm,),
        in_specs=[row, row, pl.BlockSpec((None, tm, PLE_DIM), lambda i: (layer, i, 0)),
                  _layer(w_o, o_layer), _layer(g_mlp, layer), _layer(w1, layer), _layer(w2, layer),
                  _layer(g_ple, layer), _layer(w_gate, layer), _layer(w_ple, layer), _resident((1, D_MODEL))],
        out_specs=row,
        out_shape=jax.ShapeDtypeStruct((n, D_MODEL), F32),
        compiler_params=_params("parallel"),
        name="post_block",
    )(h, o, p, w_o, g_mlp, w1, w2, g_ple, w_gate, w_ple, g_final.reshape(1, D_MODEL))


def kernel(x, p, positions, a_attn_norm, a_w_qkv, a_lambda, a_subln, a_w_o, kv_norm, kv_w, kv_b_f,
           b_attn_norm, b_w_q, b_w_o, mlp_norm, mlp_w1, mlp_w2, ple_gate_norm, ple_gate_w, ple_w,
           final_norm):
    b, s, _ = x.shape
    depth = p.shape[0]
    n_a = a_w_qkv.shape[0]
    n = b * s
    bf = lambda w: w.astype(BF16)
    gains = lambda g: g.reshape(g.shape[0], 1, D_MODEL)

    a_wqkv, a_wvt = bf(a_w_qkv), bf(jnp.swapaxes(a_w_qkv[:, :, 2 * D_MODEL:], 1, 2))
    a_wo, b_wq, b_wo = bf(a_w_o), bf(b_w_q), bf(b_w_o)
    w1, w2, w_gate, w_ple = bf(mlp_w1), bf(mlp_w2), bf(ple_gate_w), bf(ple_w)
    a_gain, b_gain, g_mlp, g_ple = gains(a_attn_norm), gains(b_attn_norm), gains(mlp_norm), gains(ple_gate_norm)
    p_rows = p.reshape(depth, n, PLE_DIM)

    tabs = _rope_tables(positions)
    h = x.reshape(n, D_MODEL)
    k_sh = vt_sh = c_t = None
    for i in range(depth):
        if i < n_a:
            q, k, vt = _diff_proj(h, a_gain, a_wqkv, a_wvt, i, tabs, b)
            lam_init = 0.8 - 0.6 * math.exp(-0.3 * i)
            o = _attention(q.reshape(b, s, D_MODEL), k.reshape(b, s, D_MODEL), vt,
                           (a_lambda[i], a_subln[i].reshape(LANES, 1)), fox=False, lam_init=lam_init)
            w_o, o_layer = a_wo, i
        else:
            j = i - n_a
            if j == 0:
                k_sh, vt_sh, f_t = _fox_kv(h, kv_norm, bf(kv_w[:, :D_MODEL]), bf(kv_w[:, D_MODEL:2 * D_MODEL].T),
                                           bf(kv_w[:, 2 * D_MODEL:].T), b)
                c_t = _forget_cumsum(f_t, kv_b_f, b).reshape(b, HEAD_PAIRS, 2, s)
                k_sh = k_sh.reshape(b, s, D_MODEL)
            q = _fox_q(h, b_gain, b_wq, j)
            o = _attention(q.reshape(b, s, D_MODEL), k_sh, vt_sh, (c_t,), fox=True)
            w_o, o_layer = b_wo, j
        h = _post(h, o.reshape(n, D_MODEL), p_rows, i, w_o, o_layer, g_mlp, w1, w2, g_ple, w_gate, w_ple,
                  final_norm, final=(i == depth - 1))
    return h.reshape(b, s, D_MODEL)
```
